```python
import math
import jax
import jax.numpy as jnp
from jax import lax
import numpy as np


D_MODEL = 2048
BATCH = 2
SEQ = 16384
DEPTH = 2

HEAD_DIM = 128
A_HEADS = 4
A_WIDTH = A_HEADS * HEAD_DIM
IDX_HEADS = 8
IDX_DIM = 64
TOPK_MAX = 256
SSM_HEADS = 16
SSM_HEAD_DIM = 64
SSM_WIDTH = SSM_HEADS * SSM_HEAD_DIM
SSM_GROUPS = 2
SSM_HEADS_PER_GROUP = SSM_HEADS // SSM_GROUPS
SSM_STATE = 128
CONV_WIDTH = 4
CONV_CH = SSM_WIDTH + 2 * SSM_GROUPS * SSM_STATE
SSD_CHUNK = 256
DIFF_HEADS = 4
DIFF_DIM = 64
DIFF_WIDTH = DIFF_HEADS * 2 * DIFF_DIM
MIX_WIDTH = A_WIDTH + SSM_WIDTH + DIFF_WIDTH
IN_SIZES = (A_WIDTH, HEAD_DIM, HEAD_DIM, IDX_HEADS * IDX_DIM, IDX_DIM, IDX_HEADS,
            SSM_WIDTH, CONV_CH, SSM_HEADS, DIFF_WIDTH, DIFF_WIDTH, DIFF_WIDTH)
IN_COLS = sum(IN_SIZES)
N_EXPERTS = 32
TOP_EXPERTS = 4
EXPERT_DIM = 2048
SWIGLU_LIMIT = 7.0
SWIGLU_ALPHA = 1.702
MOE_BLOCK = 256
Q_BLOCK = 128
ROPE_THETA = 500000.0
ROPE_FRACTION = 4
DEEPNORM_ALPHA = (2 * DEPTH) ** 0.25
DEEPNORM_BETA = (8 * DEPTH) ** -0.25
LN_EPS = 1e-5
RMS_EPS = 1e-6

kernel_name = 'hybrid_dsa_ssd_diffattn_moe_block'


def layer_norm(x, g, b):
    xf = x.astype(jnp.float32)
    mu = jnp.mean(xf, axis=-1, keepdims=True)
    var = jnp.mean(jnp.square(xf - mu), axis=-1, keepdims=True)
    return ((xf - mu) * lax.rsqrt(var + LN_EPS) * g + b).astype(x.dtype)


def rms_norm(x, g):
    xf = x.astype(jnp.float32)
    return (xf * lax.rsqrt(jnp.mean(xf * xf, axis=-1, keepdims=True) + RMS_EPS) * g).astype(x.dtype)


def rotary(x, positions):
    rot = x.shape[-1] // ROPE_FRACTION
    half = rot // 2
    inv_freq = ROPE_THETA ** (-jnp.arange(half, dtype=jnp.float32) / half)
    ang = positions.astype(jnp.float32)[:, :, None, None] * inv_freq
    cos, sin = jnp.cos(ang), jnp.sin(ang)
    x1 = x[..., :half].astype(jnp.float32)
    x2 = x[..., half:rot].astype(jnp.float32)
    return jnp.concatenate([(x1 * cos - x2 * sin).astype(x.dtype),
                            (x2 * cos + x1 * sin).astype(x.dtype),
                            x[..., rot:]], axis=-1)


def sparse_indexed_attention(q, k, v, iq, ik, iw):
    B, L = q.shape[0], q.shape[1]
    topk = min(TOPK_MAX, L // 4)
    key_pos = jnp.arange(L)

    def block(i):
        start = i * Q_BLOCK
        qpos = start + jnp.arange(Q_BLOCK)
        qb = lax.dynamic_slice_in_dim(q, start, Q_BLOCK, axis=1)
        iqb = lax.dynamic_slice_in_dim(iq, start, Q_BLOCK, axis=1)
        iwb = lax.dynamic_slice_in_dim(iw, start, Q_BLOCK, axis=1)
        rel = jax.nn.relu(jnp.einsum('bqhd,bkd->bqhk', iqb, ik))
        score = jnp.einsum('bqhk,bqh->bqk', rel, iwb).astype(jnp.float32)
        causal = key_pos[None, :] <= qpos[:, None]
        score = jnp.where(causal[None], score, -jnp.inf)
        _, sel = lax.top_k(score, topk)
        k_sel = jax.vmap(lambda kb, ib: kb[ib])(k, sel)
        v_sel = jax.vmap(lambda vb, ib: vb[ib])(v, sel)
        s = jnp.einsum('bqhd,bqkd->bhqk', qb, k_sel).astype(jnp.float32) * (HEAD_DIM ** -0.5)
        valid = sel <= qpos[None, :, None]
        s = jnp.where(valid[:, None], s, -jnp.inf)
        p = jax.nn.softmax(s, axis=-1).astype(v.dtype)
        return jnp.einsum('bhqk,bqkd->bqhd', p, v_sel)

    out = lax.map(block, jnp.arange(L // Q_BLOCK))
    return jnp.transpose(out, (1, 0, 2, 3, 4)).reshape(B, L, A_WIDTH)


def causal_depthwise_conv(x, w, b):
    y = lax.conv_general_dilated(x, w[:, None, :].astype(x.dtype), window_strides=(1,),
                                 padding=[(CONV_WIDTH - 1, 0)],
                                 dimension_numbers=('NWC', 'WIO', 'NWC'),
                                 feature_group_count=x.shape[-1])
    return y + b


def ssd_chunked_scan(x, a, b, c):
    Bsz, L, G, Hg, P = x.shape
    N = b.shape[-1]
    Q = math.gcd(SSD_CHUNK, L)
    nc = L // Q
    x = x.reshape(Bsz, nc, Q, G, Hg, P)
    a = a.reshape(Bsz, nc, Q, G, Hg)
    b = b.reshape(Bsz, nc, Q, G, N)
    c = c.reshape(Bsz, nc, Q, G, N)
    a_cum = jnp.cumsum(a, axis=2)
    causal = jnp.tril(jnp.ones((Q, Q), dtype=bool))[None, None, :, :, None, None]
    seg = a_cum[:, :, :, None] - a_cum[:, :, None, :]
    decay = jnp.exp(jnp.where(causal, seg, -jnp.inf))
    cb = jnp.einsum('bclgn,bcsgn->bclsg', c, b)
    y_diag = jnp.einsum('bclsgh,bcsghp->bclghp', cb[..., None] * decay, x)
    decay_to_end = jnp.exp(a_cum[:, :, -1:] - a_cum)
    states = jnp.einsum('bclgn,bclgh,bclghp->bcghpn', b, decay_to_end, x)
    chunk_decay = jnp.exp(a_cum[:, :, -1])

    def carry_state(h, inp):
        s_c, d_c = inp
        return h * d_c[..., None, None] + s_c, h

    h0 = jnp.zeros_like(states[:, 0])
    _, prev = lax.scan(carry_state, h0, (jnp.moveaxis(states, 1, 0), jnp.moveaxis(chunk_decay, 1, 0)))
    prev = jnp.moveaxis(prev, 0, 1)
    y_off = jnp.einsum('bclgn,bcghpn,bclgh->bclghp', c, prev, jnp.exp(a_cum))
    return (y_diag + y_off).reshape(Bsz, L, G, Hg, P)


def ssd_mixer(z, xbc, dt_raw, conv_w, conv_b, dt_bias, a_log, d_skip, norm_g):
    B, L = z.shape[0], z.shape[1]
    xbc = jax.nn.silu(causal_depthwise_conv(xbc, conv_w, conv_b))
    xs, bm, cm = jnp.split(xbc, [SSM_WIDTH, SSM_WIDTH + SSM_GROUPS * SSM_STATE], axis=-1)
    xs = xs.reshape(B, L, SSM_GROUPS, SSM_HEADS_PER_GROUP, SSM_HEAD_DIM)
    bm = bm.reshape(B, L, SSM_GROUPS, SSM_STATE)
    cm = cm.reshape(B, L, SSM_GROUPS, SSM_STATE)
    dt = jax.nn.softplus((dt_raw + dt_bias).astype(jnp.float32)).reshape(B, L, SSM_GROUPS, SSM_HEADS_PER_GROUP)
    a = -jnp.exp(a_log.astype(jnp.float32)).reshape(SSM_GROUPS, SSM_HEADS_PER_GROUP)
    y = ssd_chunked_scan(xs * dt[..., None], dt * a, bm, cm)
    y = y + d_skip.reshape(SSM_GROUPS, SSM_HEADS_PER_GROUP, 1) * xs
    y = y.reshape(B, L, SSM_WIDTH).astype(z.dtype) * jax.nn.silu(z)
    y = rms_norm(y.reshape(B, L, SSM_GROUPS, SSM_WIDTH // SSM_GROUPS),
                 norm_g.reshape(SSM_GROUPS, SSM_WIDTH // SSM_GROUPS))
    return y.reshape(B, L, SSM_WIDTH)


def differential_attention(q, k, v, lam):
    B, L = q.shape[0], q.shape[1]
    key_pos = jnp.arange(L)

    def block(i):
        start = i * Q_BLOCK
        qpos = start + jnp.arange(Q_BLOCK)
        qb = lax.dynamic_slice_in_dim(q, start, Q_BLOCK, axis=1)
        s = jnp.einsum('bqhcd,bkhcd->bhcqk', qb, k).astype(jnp.float32) * (DIFF_DIM ** -0.5)
        causal = key_pos[None, :] <= qpos[:, None]
        s = jnp.where(causal, s, -jnp.inf)
        p = jax.nn.softmax(s, axis=-1)
        amap = (p[:, :, 0] - lam * p[:, :, 1]).astype(v.dtype)
        return jnp.einsum('bhqk,bkhe->bqhe', amap, v)

    out = lax.map(block, jnp.arange(L // Q_BLOCK))
    return jnp.transpose(out, (1, 0, 2, 3, 4)).reshape(B, L, DIFF_HEADS, 2 * DIFF_DIM)


def hybrid_mixer(h, positions, layer, w_in, w_out, idx_ln_g, idx_ln_b, conv_w, conv_b,
                 dt_bias, a_log, d_skip, ssm_norm_g, diff_lambda, diff_norm_g):
    B, L, _ = h.shape
    proj = h @ w_in
    offsets = np.cumsum(IN_SIZES)[:-1].tolist()
    a_q, a_k, a_v, i_q, i_k, i_w, s_z, s_xbc, s_dt, d_q, d_k, d_v = jnp.split(proj, offsets, axis=-1)
    a_q = rotary(a_q.reshape(B, L, A_HEADS, HEAD_DIM), positions)
    a_k = rotary(a_k[:, :, None], positions)[:, :, 0]
    i_q = rotary(i_q.reshape(B, L, IDX_HEADS, IDX_DIM), positions)
    i_k = rotary(layer_norm(i_k, idx_ln_g, idx_ln_b)[:, :, None], positions)[:, :, 0]
    i_w = i_w * (IDX_HEADS ** -0.5 * IDX_DIM ** -0.5)
    out_a = sparse_indexed_attention(a_q, a_k, a_v, i_q, i_k, i_w)
    out_b = ssd_mixer(s_z, s_xbc, s_dt, conv_w, conv_b, dt_bias, a_log, d_skip, ssm_norm_g)
    lambda_init = 0.8 - 0.6 * math.exp(-0.3 * layer)
    lp = diff_lambda.astype(jnp.float32)
    lam = jnp.exp(jnp.sum(lp[0] * lp[1])) - jnp.exp(jnp.sum(lp[2] * lp[3])) + lambda_init
    d_q = rotary(d_q.reshape(B, L, 2 * DIFF_HEADS, DIFF_DIM), positions).reshape(B, L, DIFF_HEADS, 2, DIFF_DIM)
    d_k = rotary(d_k.reshape(B, L, 2 * DIFF_HEADS, DIFF_DIM), positions).reshape(B, L, DIFF_HEADS, 2, DIFF_DIM)
    out_c = differential_attention(d_q, d_k, d_v.reshape(B, L, DIFF_HEADS, 2 * DIFF_DIM), lam)
    out_c = (rms_norm(out_c, diff_norm_g) * (1.0 - lambda_init)).reshape(B, L, DIFF_WIDTH)
    return jnp.concatenate([out_a, out_b, out_c], axis=-1) @ w_out


def clamped_swiglu(gu):
    g, u = jnp.split(gu, 2, axis=-1)
    g = jnp.minimum(g, SWIGLU_LIMIT)
    u = jnp.clip(u, -SWIGLU_LIMIT, SWIGLU_LIMIT)
    return g * jax.nn.sigmoid(SWIGLU_ALPHA * g) * (u + 1.0)


def moe_ffn(h, router_w, router_b, w_gu, b_gu, w_down, b_down):
    Bsz, L, D = h.shape
    T = Bsz * L
    xt = h.reshape(T, D)
    logits = (xt @ router_w + router_b).astype(jnp.float32)
    top_val, top_idx = lax.top_k(logits, TOP_EXPERTS)
    gates = jax.nn.softmax(top_val, axis=-1)
    n_assign = T * TOP_EXPERTS
    flat_e = top_idx.reshape(-1)
    flat_tok = jnp.arange(n_assign, dtype=jnp.int32) // TOP_EXPERTS
    flat_g = gates.reshape(-1)
    order = jnp.argsort(flat_e)
    e_sorted = flat_e[order]
    counts = jnp.zeros((N_EXPERTS,), jnp.int32).at[flat_e].add(1)
    padded = ((counts + MOE_BLOCK - 1) // MOE_BLOCK) * MOE_BLOCK
    start = jnp.cumsum(counts) - counts
    pend = jnp.cumsum(padded)
    pstart = pend - padded
    dest = pstart[e_sorted] + jnp.arange(n_assign, dtype=jnp.int32) - start[e_sorted]
    n_blocks = -(-n_assign // MOE_BLOCK) + N_EXPERTS
    n_slots = n_blocks * MOE_BLOCK
    slot_tok = jnp.full((n_slots,), T, jnp.int32).at[dest].set(flat_tok[order])
    slot_gate = jnp.zeros((n_slots,), jnp.float32).at[dest].set(flat_g[order])
    block_expert = jnp.searchsorted(pend, jnp.arange(n_blocks, dtype=jnp.int32) * MOE_BLOCK, side='right')
    block_expert = jnp.minimum(block_expert, N_EXPERTS - 1)
    x_pad = jnp.concatenate([xt, jnp.zeros((1, D), xt.dtype)], axis=0)

    def expert_block(args):
        tok, e = args
        xb = x_pad[tok]
        act = clamped_swiglu(xb @ w_gu[e] + b_gu[e])
        return act @ w_down[e] + b_down[e]

    y = lax.map(expert_block, (slot_tok.reshape(n_blocks, MOE_BLOCK), block_expert))
    y = y.reshape(n_slots, D) * slot_gate[:, None].astype(y.dtype)
    out = jax.ops.segment_sum(y, slot_tok, num_segments=T + 1)[:T]
    return out.reshape(Bsz, L, D)


def setup_inputs(seed: int = 0) -> dict:
    key = jax.random.key(seed)
    ks = jax.random.split(key, 26)
    f32 = jnp.float32

    def nrm(k, shape, s):
        return jax.random.normal(k, shape, f32) * s

    x = nrm(ks[0], (BATCH, SEQ, D_MODEL), 1.0)
    c = nrm(ks[1], (BATCH, D_MODEL), 1.0)
    offset = jax.random.randint(ks[2], (BATCH, 1), 0, 4096, dtype=jnp.int32)
    positions = offset + jnp.arange(SEQ, dtype=jnp.int32)[None, :]
    w_in = nrm(ks[3], (DEPTH, D_MODEL, IN_COLS), D_MODEL ** -0.5)
    w_out = nrm(ks[4], (DEPTH, MIX_WIDTH, D_MODEL), MIX_WIDTH ** -0.5 * DEEPNORM_BETA)
    idx_ln_g = 1.0 + nrm(ks[5], (DEPTH, IDX_DIM), 0.02)
    idx_ln_b = nrm(ks[6], (DEPTH, IDX_DIM), 0.02)
    conv_w = nrm(ks[7], (DEPTH, CONV_WIDTH, CONV_CH), CONV_WIDTH ** -0.5)
    conv_b = nrm(ks[8], (DEPTH, CONV_CH), 0.02)
    dt0 = jnp.exp(jax.random.uniform(ks[9], (DEPTH, SSM_HEADS), f32, math.log(1e-3), math.log(1e-1)))
    dt_bias = dt0 + jnp.log(-jnp.expm1(-dt0))
    a_log = jnp.log(jax.random.uniform(ks[10], (DEPTH, SSM_HEADS), f32, 1.0, 16.0))
    d_skip = 1.0 + nrm(ks[11], (DEPTH, SSM_HEADS), 0.1)
    ssm_norm_g = 1.0 + nrm(ks[12], (DEPTH, SSM_WIDTH), 0.02)
    diff_lambda = nrm(ks[13], (DEPTH, 4, DIFF_DIM), 0.1)
    diff_norm_g = 1.0 + nrm(ks[14], (DEPTH, 2 * DIFF_DIM), 0.02)
    ada_w = nrm(ks[15], (DEPTH, D_MODEL, 6 * D_MODEL), 0.1 * D_MODEL ** -0.5)
    gate_offset = jnp.repeat(jnp.array([0.0, 0.0, 1.0, 0.0, 0.0, 1.0], f32), D_MODEL)
    ada_b = gate_offset + nrm(ks[16], (DEPTH, 6 * D_MODEL), 0.02)
    ln_g = 1.0 + nrm(ks[17], (DEPTH, 2, D_MODEL), 0.02)
    ln_b = nrm(ks[18], (DEPTH, 2, D_MODEL), 0.02)
    router_w = nrm(ks[19], (DEPTH, D_MODEL, N_EXPERTS), D_MODEL ** -0.5)
    router_b = nrm(ks[20], (DEPTH, N_EXPERTS), 0.01)
    w_gu = nrm(ks[21], (DEPTH, N_EXPERTS, D_MODEL, 2 * EXPERT_DIM), D_MODEL ** -0.5)
    b_gu = nrm(ks[22], (DEPTH, N_EXPERTS, 2 * EXPERT_DIM), 0.02)
    w_down = nrm(ks[23], (DEPTH, N_EXPERTS, EXPERT_DIM, D_MODEL), EXPERT_DIM ** -0.5 * DEEPNORM_BETA)
    b_down = nrm(ks[24], (DEPTH, N_EXPERTS, D_MODEL), 0.02)
    return {'x': x, 'c': c, 'positions': positions, 'w_in': w_in, 'w_out': w_out,
            'idx_ln_g': idx_ln_g, 'idx_ln_b': idx_ln_b, 'conv_w': conv_w, 'conv_b': conv_b,
            'dt_bias': dt_bias, 'a_log': a_log, 'd_skip': d_skip, 'ssm_norm_g': ssm_norm_g,
            'diff_lambda': diff_lambda, 'diff_norm_g': diff_norm_g, 'ada_w': ada_w, 'ada_b': ada_b,
            'ln_g': ln_g, 'ln_b': ln_b, 'router_w': router_w, 'router_b': router_b,
            'w_gu': w_gu, 'b_gu': b_gu, 'w_down': w_down, 'b_down': b_down}


def reference(x, c, positions, w_in, w_out, idx_ln_g, idx_ln_b, conv_w, conv_b, dt_bias, a_log,
              d_skip, ssm_norm_g, diff_lambda, diff_norm_g, ada_w, ada_b, ln_g, ln_b,
              router_w, router_b, w_gu, b_gu, w_down, b_down):
    c_act = jax.nn.silu(c)
    for layer in range(DEPTH):
        mod = c_act @ ada_w[layer] + ada_b[layer]
        shift_m, scale_m, gate_m, shift_f, scale_f, gate_f = jnp.split(mod[:, None, :], 6, axis=-1)
        h = x * (1.0 + scale_m) + shift_m
        y = hybrid_mixer(h, positions, layer, w_in[layer], w_out[layer], idx_ln_g[layer], idx_ln_b[layer],
                         conv_w[layer], conv_b[layer], dt_bias[layer], a_log[layer], d_skip[layer],
                         ssm_norm_g[layer], diff_lambda[layer], diff_norm_g[layer])
        x = layer_norm(DEEPNORM_ALPHA * x + gate_m * y, ln_g[layer, 0], ln_b[layer, 0])
        h = x * (1.0 + scale_f) + shift_f
        y = moe_ffn(h, router_w[layer], router_b[layer], w_gu[layer], b_gu[layer], w_down[layer], b_down[layer])
        x = layer_norm(DEEPNORM_ALPHA * x + gate_f * y, ln_g[layer, 1], ln_b[layer, 1])
    return x
```

```python
import functools
import math

import jax
import jax.numpy as jnp
from jax import lax
from jax.experimental import pallas as pl
from jax.experimental.pallas import tpu as pltpu

HEAD_DIM = 128
A_HEADS = 4
A_WIDTH = A_HEADS * HEAD_DIM
IDX_HEADS = 8
IDX_DIM = 64
TOPK_MAX = 256
SSM_HEADS = 16
SSM_HEAD_DIM = 64
SSM_WIDTH = SSM_HEADS * SSM_HEAD_DIM
SSM_GROUPS = 2
SSM_HEADS_PER_GROUP = SSM_HEADS // SSM_GROUPS
SSM_STATE = 128
CONV_WIDTH = 4
CONV_CH = SSM_WIDTH + 2 * SSM_GROUPS * SSM_STATE
SSD_CHUNK = 256
DIFF_HEADS = 4
DIFF_DIM = 64
DIFF_WIDTH = DIFF_HEADS * 2 * DIFF_DIM
TOP_EXPERTS = 4
SWIGLU_LIMIT = 7.0
SWIGLU_ALPHA = 1.702
ROPE_THETA = 500000.0
LN_EPS = 1e-5
RMS_EPS = 1e-6

LANES = 128
VMEM_LIMIT = 56 * 1024 * 1024
NEG_BIG = -1e30
INT_MIN = -(2 ** 31)

DSA_IN = 11 * LANES
DSA_OUT = 12 * LANES
SSM_COLS = SSM_WIDTH + CONV_CH + SSM_HEADS
SSM_PAD = 21 * LANES
DIFF_COLS = 3 * DIFF_WIDTH

bf16 = jnp.bfloat16
f32 = jnp.float32


def _cparams(sem, vmem=VMEM_LIMIT):
    return pltpu.CompilerParams(dimension_semantics=sem, vmem_limit_bytes=vmem)


def _split3(a):
    a1 = a.astype(bf16)
    r = a - a1.astype(f32)
    a2 = r.astype(bf16)
    a3 = (r - a2.astype(f32)).astype(bf16)
    return a1, a2, a3


def _dot(a, b):
    return jnp.dot(a, b, preferred_element_type=f32)


def _dot_nt(a, b):
    return lax.dot_general(a, b, (((1,), (1,)), ((), ())), preferred_element_type=f32)


def _dot_tn(a, b):
    return lax.dot_general(a, b, (((0,), (0,)), ((), ())), preferred_element_type=f32)


def _dot_f32(a, b3):
    a1, a2, a3 = _split3(a)
    b1, b2, b3_ = b3
    return (_dot(a1, b1) + (_dot(a1, b2) + _dot(a2, b1))
            + (_dot(a2, b2) + _dot(a1, b3_) + _dot(a3, b1)))


def _ada_kernel(c_ref, w_ref, b_ref, o_ref):
    c = c_ref[...]
    c = c * (1.0 / (1.0 + jnp.exp(-c)))
    w1, w2, w3 = _split3(w_ref[0])
    o_ref[0] = _dot_f32(c, (w1, w2, w3)) + b_ref[0]


def _ada_mod(c, ada_w, ada_b):
    depth, d, n = ada_w.shape
    bsz = c.shape[0]
    rows = 8
    cp = jnp.zeros((rows, d), f32).at[:bsz].set(c)
    tn = 512
    out = pl.pallas_call(
        _ada_kernel,
        grid=(depth, n // tn),
        in_specs=[pl.BlockSpec((rows, d), lambda l, j: (0, 0)),
                  pl.BlockSpec((1, d, tn), lambda l, j: (l, 0, j)),
                  pl.BlockSpec((1, 1, tn), lambda l, j: (l, 0, j))],
        out_specs=pl.BlockSpec((1, rows, tn), lambda l, j: (l, 0, j)),
        out_shape=jax.ShapeDtypeStruct((depth, rows, n), f32),
        compiler_params=_cparams(("parallel", "parallel")),
        name="ada_mod",
    )(cp, ada_w, ada_b.reshape(depth, 1, n))
    return out[:, :bsz]


def _rope_tables(positions, head_dim):
    rot = head_dim // 4
    half = rot // 2
    pos = positions.reshape(-1).astype(f32)
    inv_freq = ROPE_THETA ** (-jnp.arange(half, dtype=f32) / half)
    ang = pos[:, None] * inv_freq[None, :]
    cos, sin = jnp.cos(ang), jnp.sin(ang)
    lane = jnp.arange(LANES)
    q = lane % head_dim
    j = q % half
    in_rot = q < rot
    first = q < half
    c = jnp.where(in_rot[None, :], cos[:, j], 1.0)
    s1 = jnp.where(first[None, :], -sin[:, j], 0.0)
    s2 = jnp.where((in_rot & ~first)[None, :], sin[:, j], 0.0)
    return c.astype(f32), s1.astype(f32), s2.astype(f32)


def _rot_block(blk, c, s1, s2, half):
    return (blk * c + pltpu.roll(blk, LANES - half, 1) * s1
            + pltpu.roll(blk, half, 1) * s2)


def _modulate(x_ref, sc_ref, sh_ref):
    return (x_ref[...] * (1.0 + sc_ref[0]) + sh_ref[0]).astype(bf16)


def _proj_dsa_kernel(x_ref, sc_ref, sh_ref, w_ref, c128, s1_128, s2_128, c64, s1_64, s2_64,
                     lng_ref, lnb_ref, o_ref, iw_ref):
    h = _modulate(x_ref, sc_ref, sh_ref)
    acc = _dot(h, w_ref[...])
    t128 = (c128[...], s1_128[...], s2_128[...])
    t64 = (c64[...], s1_64[...], s2_64[...])
    a_scale = HEAD_DIM ** -0.5
    for cb in range(4):
        blk = acc[:, cb * LANES:(cb + 1) * LANES]
        o_ref[:, cb * LANES:(cb + 1) * LANES] = (_rot_block(blk, *t128, 16) * a_scale).astype(bf16)
    for cb in range(4, 8):
        blk = acc[:, cb * LANES:(cb + 1) * LANES]
        o_ref[:, cb * LANES:(cb + 1) * LANES] = _rot_block(blk, *t64, 8).astype(bf16)
    blk = acc[:, 8 * LANES:9 * LANES]
    o_ref[:, 8 * LANES:9 * LANES] = _rot_block(blk, *t128, 16).astype(bf16)
    o_ref[:, 9 * LANES:10 * LANES] = acc[:, 9 * LANES:10 * LANES].astype(bf16)
    blk = acc[:, 10 * LANES:11 * LANES]
    lane = lax.broadcasted_iota(jnp.int32, blk.shape, 1)
    is_k = lane < IDX_DIM
    mu = jnp.sum(jnp.where(is_k, blk, 0.0), axis=1, keepdims=True) * (1.0 / IDX_DIM)
    d = jnp.where(is_k, blk - mu, 0.0)
    var = jnp.sum(d * d, axis=1, keepdims=True) * (1.0 / IDX_DIM)
    kn = d * lax.rsqrt(var + LN_EPS) * lng_ref[...] + lnb_ref[...]
    kn = jnp.where(is_k, _rot_block(kn, *t64, 8), 0.0)
    o_ref[:, 10 * LANES:11 * LANES] = kn.astype(bf16)
    o_ref[:, 11 * LANES:12 * LANES] = pltpu.roll(kn, IDX_DIM, 1).astype(bf16)
    iw_ref[...] = pltpu.roll(blk, IDX_DIM, 1) * (IDX_HEADS ** -0.5 * IDX_DIM ** -0.5)


def _proj_ssm_kernel(x_ref, sc_ref, sh_ref, w_ref, o_ref):
    h = _modulate(x_ref, sc_ref, sh_ref)
    o_ref[...] = _dot(h, w_ref[...])


def _proj_diff_kernel(x_ref, sc_ref, sh_ref, w_ref, c64, s1_64, s2_64, o_ref):
    h = _modulate(x_ref, sc_ref, sh_ref)
    acc = _dot(h, w_ref[...])
    t64 = (c64[...], s1_64[...], s2_64[...])
    q_scale = DIFF_DIM ** -0.5
    for cb in range(4):
        blk = acc[:, cb * LANES:(cb + 1) * LANES]
        o_ref[:, cb * LANES:(cb + 1) * LANES] = (_rot_block(blk, *t64, 8) * q_scale).astype(bf16)
    for cb in range(4, 8):
        blk = acc[:, cb * LANES:(cb + 1) * LANES]
        o_ref[:, cb * LANES:(cb + 1) * LANES] = _rot_block(blk, *t64, 8).astype(bf16)
    o_ref[:, 8 * LANES:] = acc[:, 8 * LANES:].astype(bf16)


def _row_specs(tm, d, tiles_per_batch):
    return [pl.BlockSpec((tm, d), lambda i: (i, 0)),
            pl.BlockSpec((1, 1, d), lambda i: (i // tiles_per_batch, 0, 0)),
            pl.BlockSpec((1, 1, d), lambda i: (i // tiles_per_batch, 0, 0))]


def _tab_spec(tm):
    return pl.BlockSpec((tm, LANES), lambda i: (i, 0))


def _full_spec(shape):
    nd = len(shape)
    return pl.BlockSpec(shape, lambda i: (0,) * nd)


def _proj_dsa(x2, scale, shift, w, t128, t64, lng, lnb, seq):
    t, d = x2.shape
    tm = 256
    return pl.pallas_call(
        _proj_dsa_kernel,
        grid=(t // tm,),
        in_specs=_row_specs(tm, d, seq // tm) + [_full_spec(w.shape)] + [_tab_spec(tm)] * 6
        + [_full_spec((1, LANES))] * 2,
        out_specs=[pl.BlockSpec((tm, DSA_OUT), lambda i: (i, 0)),
                   pl.BlockSpec((tm, LANES), lambda i: (i, 0))],
        out_shape=[jax.ShapeDtypeStruct((t, DSA_OUT), bf16),
                   jax.ShapeDtypeStruct((t, LANES), f32)],
        compiler_params=_cparams(("parallel",)),
        name="proj_dsa",
    )(x2, scale, shift, w, *t128, *t64, lng, lnb)


def _proj_ssm(x2, scale, shift, w, seq):
    t, d = x2.shape
    tm = 256
    return pl.pallas_call(
        _proj_ssm_kernel,
        grid=(t // tm,),
        in_specs=_row_specs(tm, d, seq // tm) + [_full_spec(w.shape)],
        out_specs=pl.BlockSpec((tm, SSM_PAD), lambda i: (i, 0)),
        out_shape=jax.ShapeDtypeStruct((t, SSM_PAD), f32),
        compiler_params=_cparams(("parallel",)),
        name="proj_ssm",
    )(x2, scale, shift, w)


def _proj_diff(x2, scale, shift, w, t64, seq):
    t, d = x2.shape
    tm = 256
    return pl.pallas_call(
        _proj_diff_kernel,
        grid=(t // tm,),
        in_specs=_row_specs(tm, d, seq // tm) + [_full_spec(w.shape)] + [_tab_spec(tm)] * 3,
        out_specs=pl.BlockSpec((tm, DIFF_COLS), lambda i: (i, 0)),
        out_shape=jax.ShapeDtypeStruct((t, DIFF_COLS), bf16),
        compiler_params=_cparams(("parallel",)),
        name="proj_diff",
    )(x2, scale, shift, w, *t64)


def _split_w_in(w_in_l, lng, lnb):
    d = w_in_l.shape[0]
    sizes = (A_WIDTH, HEAD_DIM, HEAD_DIM, IDX_HEADS * IDX_DIM, IDX_DIM, IDX_HEADS,
             SSM_WIDTH, CONV_CH, SSM_HEADS, DIFF_WIDTH, DIFF_WIDTH, DIFF_WIDTH)
    offs = [0]
    for s in sizes:
        offs.append(offs[-1] + s)
    col = lambda k: w_in_l[:, offs[k]:offs[k + 1]]
    pad = lambda n: jnp.zeros((d, n), w_in_l.dtype)
    w_dsa = jnp.concatenate([col(0), col(3), col(1), col(2), col(4), col(5),
                             pad(LANES - IDX_DIM - IDX_HEADS)], axis=1).astype(bf16)
    w_ssm = jnp.concatenate([col(6), col(7), col(8), pad(SSM_PAD - SSM_COLS)], axis=1).astype(bf16)
    w_diff = jnp.concatenate([col(9), col(10), col(11)], axis=1).astype(bf16)
    lng_p = jnp.zeros((1, LANES), f32).at[0, :IDX_DIM].set(lng)
    lnb_p = jnp.zeros((1, LANES), f32).at[0, :IDX_DIM].set(lnb)
    return w_dsa, w_ssm, w_diff, lng_p, lnb_p


def _sortable(x):
    b = lax.bitcast_convert_type(x, jnp.int32)
    return b ^ ((b >> 31) & jnp.int32(0x7FFFFFFF))


def _online_softmax_step(s, v_blk, carry):
    m, l, acc = carry
    m_new = jnp.maximum(m, jnp.max(s, axis=1, keepdims=True))
    alpha = jnp.exp(m - m_new)
    p = jnp.exp(s - m_new)
    l = alpha * l + jnp.sum(p, axis=1, keepdims=True)
    acc = alpha * acc + _dot(p.astype(bf16), v_blk)
    return m_new, l, acc


def _dsa_kernel(aq_ref, iq_ref, ak_ref, av_ref, ika_ref, ikb_ref, iw_ref, o_ref, keys_ref,
                *, tq, tk, topk):
    i = pl.program_id(1)
    q_start = i * tq
    n_chunks = (q_start + tq + tk - 1) // tk
    row = q_start + lax.broadcasted_iota(jnp.int32, (tq, tk), 0)
    iw = iw_ref[...]
    w_cols = [iw[:, h:h + 1] for h in range(IDX_HEADS)]

    def score_chunk(kc, carry):
        off = pl.multiple_of(kc * tk, tk)
        ka = ika_ref[pl.ds(off, tk), :]
        kb = ikb_ref[pl.ds(off, tk), :]
        sc = jnp.zeros((tq, tk), f32)
        for m in range(IDX_HEADS // 2):
            qp = iq_ref[:, m * LANES:(m + 1) * LANES]
            sc = sc + w_cols[2 * m] * jnp.maximum(_dot_nt(qp, ka), 0.0)
            sc = sc + w_cols[2 * m + 1] * jnp.maximum(_dot_nt(qp, kb), 0.0)
        col = off + lax.broadcasted_iota(jnp.int32, (tq, tk), 1)
        keys_ref[:, pl.ds(off, tk)] = jnp.where(col <= row, _sortable(sc), INT_MIN)
        return carry

    lax.fori_loop(0, n_chunks, score_chunk, 0)

    def count_ge(cand):
        def body(kc, part):
            off = pl.multiple_of(kc * tk, tk)
            hit = jnp.where(keys_ref[:, pl.ds(off, tk)] >= cand, 1.0, 0.0)
            for c in range(tk // LANES):
                part = part + hit[:, c * LANES:(c + 1) * LANES]
            return part
        part = lax.fori_loop(0, n_chunks, body, jnp.zeros((tq, LANES), f32))
        return jnp.sum(part, axis=1, keepdims=True)

    zero = jnp.zeros((tq, 1), jnp.int32)
    t0 = jnp.where(count_ge(zero) >= topk, zero, zero + INT_MIN)

    def bit_body(b, t):
        cand = t + jnp.left_shift(jnp.int32(1), 30 - b)
        return jnp.where(count_ge(cand) >= topk, cand, t)

    thr = jnp.maximum(lax.fori_loop(0, 31, bit_body, t0), INT_MIN + 1)

    q4 = jnp.concatenate([aq_ref[:, h * LANES:(h + 1) * LANES] for h in range(A_HEADS)], axis=0)

    def attn_chunk(kc, carry):
        off = pl.multiple_of(kc * tk, tk)
        bias = jnp.where(keys_ref[:, pl.ds(off, tk)] >= thr, 0.0, NEG_BIG)
        s = _dot_nt(q4, ak_ref[pl.ds(off, tk), :]) + jnp.concatenate([bias] * A_HEADS, axis=0)
        return _online_softmax_step(s, av_ref[pl.ds(off, tk), :], carry)

    init = (jnp.full((A_HEADS * tq, 1), NEG_BIG, f32), jnp.zeros((A_HEADS * tq, 1), f32),
            jnp.zeros((A_HEADS * tq, LANES), f32))
    _, l, acc = lax.fori_loop(0, n_chunks, attn_chunk, init)
    o = acc / l
    for h in range(A_HEADS):
        o_ref[:, h * LANES:(h + 1) * LANES] = o[h * tq:(h + 1) * tq].astype(bf16)


def _dsa_attention(qkv, iw, bsz, seq):
    tq = min(128, seq)
    tk = min(512, seq)
    nq = seq // tq
    topk = min(TOPK_MAX, seq // 4)
    qspec = lambda c: pl.BlockSpec((tq, A_WIDTH), lambda b, i: (b * nq + i, c))
    kspec = lambda c: pl.BlockSpec((seq, LANES), lambda b, i: (b, c))
    return pl.pallas_call(
        functools.partial(_dsa_kernel, tq=tq, tk=tk, topk=topk),
        grid=(bsz, nq),
        in_specs=[qspec(0), qspec(1), kspec(8), kspec(9), kspec(10), kspec(11),
                  pl.BlockSpec((tq, LANES), lambda b, i: (b * nq + i, 0))],
        out_specs=pl.BlockSpec((tq, A_WIDTH), lambda b, i: (b * nq + i, 0)),
        out_shape=jax.ShapeDtypeStruct((bsz * seq, A_WIDTH), bf16),
        scratch_shapes=[pltpu.VMEM((tq, seq), jnp.int32)],
        compiler_params=_cparams(("parallel", "arbitrary")),
        name="dsa_attention",
    )(qkv, qkv, qkv, qkv, qkv, qkv, iw)


def _diff_kernel(q_ref, k_ref, v_ref, lam_ref, g_ref, o_ref, *, tq, tk, lam_init):
    i = pl.program_id(2)
    q_start = i * tq
    n_chunks = (q_start + tq + tk - 1) // tk
    qf = q_ref[...].astype(f32)
    lane = lax.broadcasted_iota(jnp.int32, qf.shape, 1)
    q2 = jnp.concatenate([jnp.where(lane < DIFF_DIM, qf, 0.0),
                          jnp.where(lane >= DIFF_DIM, qf, 0.0)], axis=0).astype(bf16)
    row = q_start + lax.broadcasted_iota(jnp.int32, (tq, tk), 0)
    row2 = jnp.concatenate([row, row], axis=0)

    def chunk(kc, carry):
        off = pl.multiple_of(kc * tk, tk)
        s = _dot_nt(q2, k_ref[pl.ds(off, tk), :])
        col = off + lax.broadcasted_iota(jnp.int32, (2 * tq, tk), 1)
        s = jnp.where(col <= row2, s, NEG_BIG)
        return _online_softmax_step(s, v_ref[pl.ds(off, tk), :], carry)

    init = (jnp.full((2 * tq, 1), NEG_BIG, f32), jnp.zeros((2 * tq, 1), f32),
            jnp.zeros((2 * tq, LANES), f32))
    _, l, acc = lax.fori_loop(0, n_chunks, chunk, init)
    o = acc / l
    lp = lam_ref[...]
    lam = (jnp.exp(jnp.sum(lp[0:1] * lp[1:2], axis=1, keepdims=True))
           - jnp.exp(jnp.sum(lp[2:3] * lp[3:4], axis=1, keepdims=True)) + lam_init)
    o = o[:tq] - lam * o[tq:]
    ms = jnp.mean(o * o, axis=1, keepdims=True)
    o_ref[...] = (o * lax.rsqrt(ms + RMS_EPS) * g_ref[...] * (1.0 - lam_init)).astype(bf16)


def _diff_attention(qkv, diff_lambda_l, diff_norm_g_l, lam_init, bsz, seq):
    tq = min(256, seq)
    tk = min(512, seq)
    nq = seq // tq
    return pl.pallas_call(
        functools.partial(_diff_kernel, tq=tq, tk=tk, lam_init=lam_init),
        grid=(bsz, DIFF_HEADS, nq),
        in_specs=[pl.BlockSpec((tq, LANES), lambda b, h, i: (b * nq + i, h)),
                  pl.BlockSpec((seq, LANES), lambda b, h, i: (b, DIFF_HEADS + h)),
                  pl.BlockSpec((seq, LANES), lambda b, h, i: (b, 2 * DIFF_HEADS + h)),
                  pl.BlockSpec((4, DIFF_DIM), lambda b, h, i: (0, 0)),
                  pl.BlockSpec((1, LANES), lambda b, h, i: (0, 0))],
        out_specs=pl.BlockSpec((tq, LANES), lambda b, h, i: (b * nq + i, h)),
        out_shape=jax.ShapeDtypeStruct((bsz * seq, DIFF_WIDTH), bf16),
        compiler_params=_cparams(("parallel", "parallel", "arbitrary")),
        name="diff_attention",
    )(qkv, qkv, qkv, diff_lambda_l, diff_norm_g_l.reshape(1, LANES))


def _sigmoid(x):
    return 1.0 / (1.0 + jnp.exp(-x))


def _expand_heads(v, e_ref):
    v1, v2, v3 = _split3(v)
    e = e_ref[...]
    return _dot(v1, e) + _dot(v2, e) + _dot(v3, e)


def _ssd_kernel(s_ref, cw_ref, cb_ref, dtb_ref, alog_ref, dskip_ref, ng_ref, e_ref, o_ref,
                xpad_ref, state_ref, *, q):
    hp = SSM_HEADS_PER_GROUP * SSM_HEAD_DIM
    xo, do = SSM_WIDTH, SSM_WIDTH + CONV_CH

    @pl.when(pl.program_id(1) == 0)
    def _():
        xpad_ref[0:8, :] = jnp.zeros((8, CONV_CH), f32)
        state_ref[...] = jnp.zeros(state_ref.shape, f32)

    xpad_ref[8:, :] = s_ref[:, xo:do]
    cw = cw_ref[...]
    y = cb_ref[...] + cw[3:4] * xpad_ref[8:q + 8, :]
    y = y + cw[2:3] * xpad_ref[7:q + 7, :]
    y = y + cw[1:2] * xpad_ref[6:q + 6, :]
    y = y + cw[0:1] * xpad_ref[5:q + 5, :]
    xpad_ref[0:8, :] = xpad_ref[q:q + 8, :]
    xa = y * _sigmoid(y)
    xs = xa[:, :SSM_WIDTH]

    dtr = s_ref[:, do:do + LANES] + dtb_ref[...]
    dt = jnp.maximum(dtr, 0.0) + jnp.log1p(jnp.exp(-jnp.abs(dtr)))
    a = dt * (-jnp.exp(alog_ref[...]))
    ri = lax.broadcasted_iota(jnp.int32, (q, q), 0)
    ci = lax.broadcasted_iota(jnp.int32, (q, q), 1)
    tril = ri >= ci
    tri = jnp.where(tril, 1.0, 0.0).astype(bf16)
    a1, a2, a3 = _split3(a)
    a_cum = _dot(tri, a1) + _dot(tri, a2) + _dot(tri, a3)
    a_cum_t = a_cum.T
    a_last = a_cum[q - 1:q, :]
    dt_x = _expand_heads(dt, e_ref)
    ea_x = _expand_heads(jnp.exp(a_cum), e_ref)
    dte_x = _expand_heads(jnp.exp(a_last - a_cum), e_ref)
    xdt = xs * dt_x
    xw = (xdt * dte_x).astype(bf16)
    lane = lax.broadcasted_iota(jnp.int32, (q, LANES), 1)
    lo_half = lane < SSM_HEAD_DIM

    outs = []
    for g in range(SSM_GROUPS):
        bg = xa[:, SSM_WIDTH + g * SSM_STATE:SSM_WIDTH + (g + 1) * SSM_STATE].astype(bf16)
        cg = xa[:, SSM_WIDTH + (SSM_GROUPS + g) * SSM_STATE:
                SSM_WIDTH + (SSM_GROUPS + g + 1) * SSM_STATE].astype(bf16)
        cbm = _dot_nt(cg, bg)
        pieces = []
        for pr in range(SSM_HEADS_PER_GROUP // 2):
            c0 = g * hp + pr * LANES
            xpair = xdt[:, c0:c0 + LANES]
            acc = None
            for sub in range(2):
                h = g * SSM_HEADS_PER_GROUP + 2 * pr + sub
                seg = a_cum[:, h:h + 1] - a_cum_t[h:h + 1, :]
                dec = jnp.where(tril, jnp.exp(seg), 0.0)
                mm = (cbm * dec).astype(bf16)
                keep = lo_half if sub == 0 else jnp.logical_not(lo_half)
                part = _dot(mm, jnp.where(keep, xpair, 0.0).astype(bf16))
                acc = part if acc is None else acc + part
            pieces.append(acc)
        y_diag = jnp.concatenate(pieces, axis=1)
        prev = state_ref[g]
        y_off = _dot(cg, prev.astype(bf16)) * ea_x[:, g * hp:(g + 1) * hp]
        new_states = _dot_tn(bg, xw[:, g * hp:(g + 1) * hp])
        state_ref[g] = prev * ea_x[q - 1:q, g * hp:(g + 1) * hp] + new_states
        outs.append(y_diag + y_off)
    yv = jnp.concatenate(outs, axis=1) + dskip_ref[...] * xs
    z = s_ref[:, :SSM_WIDTH]
    yv = yv * (z * _sigmoid(z))
    ng = ng_ref[...]
    for g in range(SSM_GROUPS):
        yg = yv[:, g * hp:(g + 1) * hp]
        ms = jnp.mean(yg * yg, axis=1, keepdims=True)
        o_ref[:, g * hp:(g + 1) * hp] = (yg * lax.rsqrt(ms + RMS_EPS)
                                         * ng[:, g * hp:(g + 1) * hp]).astype(bf16)


def _ssd_mixer(ssm, conv_w, conv_b, dt_bias, a_log, d_skip, norm_g, bsz, seq):
    q = math.gcd(SSD_CHUNK, seq)
    nc = seq // q
    pad16 = lambda v: jnp.zeros((1, LANES), f32).at[0, :SSM_HEADS].set(v)
    head_of_lane = jnp.arange(SSM_WIDTH) // SSM_HEAD_DIM
    expand = (jnp.arange(LANES)[:, None] == head_of_lane[None, :]).astype(bf16)
    const = lambda shape: pl.BlockSpec(shape, lambda b, c: (0,) * len(shape))
    return pl.pallas_call(
        functools.partial(_ssd_kernel, q=q),
        grid=(bsz, nc),
        in_specs=[pl.BlockSpec((q, SSM_PAD), lambda b, c: (b * nc + c, 0)),
                  const((CONV_WIDTH, CONV_CH)), const((1, CONV_CH)), const((1, LANES)),
                  const((1, LANES)), const((1, SSM_WIDTH)), const((1, SSM_WIDTH)),
                  const((LANES, SSM_WIDTH))],
        out_specs=pl.BlockSpec((q, SSM_WIDTH), lambda b, c: (b * nc + c, 0)),
        out_shape=jax.ShapeDtypeStruct((bsz * seq, SSM_WIDTH), bf16),
        scratch_shapes=[pltpu.VMEM((q + 8, CONV_CH), f32),
                        pltpu.VMEM((SSM_GROUPS, SSM_STATE, SSM_HEADS_PER_GROUP * SSM_HEAD_DIM), f32)],
        compiler_params=_cparams(("parallel", "arbitrary")),
        name="ssd_mixer",
    )(ssm, conv_w, conv_b.reshape(1, CONV_CH), pad16(dt_bias), pad16(a_log),
      jnp.repeat(d_skip, SSM_HEAD_DIM).reshape(1, SSM_WIDTH), norm_g.reshape(1, SSM_WIDTH), expand)


def _layer_norm_rows(r, g, b):
    mu = jnp.mean(r, axis=1, keepdims=True)
    d = r - mu
    var = jnp.mean(d * d, axis=1, keepdims=True)
    return d * lax.rsqrt(var + LN_EPS) * g + b


def _outproj_kernel(oa_ref, ob_ref, oc_ref, w_ref, x_ref, gate_ref, g_ref, b_ref, o_ref, *, alpha):
    y = _dot(oa_ref[...], w_ref[0:A_WIDTH, :])
    y = y + _dot(ob_ref[...], w_ref[A_WIDTH:A_WIDTH + SSM_WIDTH, :])
    y = y + _dot(oc_ref[...], w_ref[A_WIDTH + SSM_WIDTH:, :])
    r = alpha * x_ref[...] + gate_ref[0] * y
    o_ref[...] = _layer_norm_rows(r, g_ref[...], b_ref[...])


def _out_proj(oa, ob, oc, w_out_b, x2, gate, ln_g, ln_b, alpha, seq):
    t, d = x2.shape
    tm = 256
    tpb = seq // tm
    rows = lambda w: pl.BlockSpec((tm, w), lambda i: (i, 0))
    return pl.pallas_call(
        functools.partial(_outproj_kernel, alpha=alpha),
        grid=(t // tm,),
        in_specs=[rows(A_WIDTH), rows(SSM_WIDTH), rows(DIFF_WIDTH), _full_spec(w_out_b.shape),
                  rows(d), pl.BlockSpec((1, 1, d), lambda i: (i // tpb, 0, 0)),
                  _full_spec((1, d)), _full_spec((1, d))],
        out_specs=rows(d),
        out_shape=jax.ShapeDtypeStruct((t, d), f32),
        compiler_params=_cparams(("parallel",)),
        name="out_proj_ln",
    )(oa, ob, oc, w_out_b, x2, gate, ln_g.reshape(1, d), ln_b.reshape(1, d))


def _router_kernel(x_ref, sc_ref, sh_ref, w1_ref, w2_ref, w3_ref, b_ref, h_ref, idx_ref, gate_ref,
                   *, n_experts):
    h = x_ref[...] * (1.0 + sc_ref[0]) + sh_ref[0]
    h_ref[...] = h.astype(bf16)
    logits = _dot_f32(h, (w1_ref[...], w2_ref[...], w3_ref[...])) + b_ref[...]
    lane = lax.broadcasted_iota(jnp.int32, logits.shape, 1)
    lane_f = lane.astype(f32)
    cur = jnp.where(lane < n_experts, logits, -jnp.inf)
    vals, idxs = [], []
    for _ in range(TOP_EXPERTS):
        m = jnp.max(cur, axis=1, keepdims=True)
        ix = jnp.min(jnp.where(cur == m, lane_f, float(LANES)), axis=1, keepdims=True)
        vals.append(m)
        idxs.append(ix)
        cur = jnp.where(lane_f == ix, -jnp.inf, cur)
    es = [jnp.exp(v - vals[0]) for v in vals]
    tot = es[0] + es[1] + es[2] + es[3]
    gates = jnp.zeros(logits.shape, f32)
    idx_o = jnp.zeros(logits.shape, f32)
    for k in range(TOP_EXPERTS):
        gates = jnp.where(lane == k, es[k] / tot, gates)
        idx_o = jnp.where(lane == k, idxs[k], idx_o)
    gate_ref[...] = gates
    idx_ref[...] = idx_o.astype(jnp.int32)


def _router(x2, scale, shift, router_w_l, router_b_l, seq):
    t, d = x2.shape
    n_experts = router_w_l.shape[1]
    tm = 256
    wp = jnp.zeros((d, LANES), f32).at[:, :n_experts].set(router_w_l)
    w1, w2, w3 = _split3(wp)
    bp = jnp.zeros((1, LANES), f32).at[0, :n_experts].set(router_b_l)
    rows = lambda w: pl.BlockSpec((tm, w), lambda i: (i, 0))
    return pl.pallas_call(
        functools.partial(_router_kernel, n_experts=n_experts),
        grid=(t // tm,),
        in_specs=_row_specs(tm, d, seq // tm) + [_full_spec((d, LANES))] * 3 + [_full_spec((1, LANES))],
        out_specs=[rows(d), rows(LANES), rows(LANES)],
        out_shape=[jax.ShapeDtypeStruct((t, d), bf16),
                   jax.ShapeDtypeStruct((t, LANES), jnp.int32),
                   jax.ShapeDtypeStruct((t, LANES), f32)],
        compiler_params=_cparams(("parallel",)),
        name="router",
    )(x2, scale, shift, w1, w2, w3, bp)


MOE_TM = 512
MOE_TH = 512


def _ffn_kernel(be_ref, nb_ref, x_ref, wg_ref, wu_ref, wd_ref, bg_ref, bu_ref, bd_ref, gate_ref,
                o_ref, acc_ref, *, nj):
    blk = pl.program_id(0)
    j = pl.program_id(1)
    live = blk < nb_ref[0]

    @pl.when(live)
    def _():
        x = x_ref[...]
        g = _dot(x, wg_ref[0]) + bg_ref[0]
        u = _dot(x, wu_ref[0]) + bu_ref[0]
        g = jnp.minimum(g, SWIGLU_LIMIT)
        u = jnp.clip(u, -SWIGLU_LIMIT, SWIGLU_LIMIT)
        act = (g * _sigmoid(SWIGLU_ALPHA * g) * (u + 1.0)).astype(bf16)
        part = _dot(act, wd_ref[0])

        @pl.when(j == 0)
        def _():
            acc_ref[...] = part

        @pl.when(j > 0)
        def _():
            acc_ref[...] += part

    @pl.when(j == nj - 1)
    def _():
        y = (acc_ref[...] + bd_ref[0]) * gate_ref[...]
        o_ref[...] = jnp.where(live, y, 0.0).astype(bf16)


def _expert_ffn(xs, block_expert, n_live, w_gu_b, b_gu_l, w_down_b, b_down_l, slot_gate):
    n_slots, d = xs.shape
    n_exp, _, two_f = w_gu_b.shape
    fdim = two_f // 2
    tm, th = MOE_TM, min(MOE_TH, fdim)
    nj = fdim // th
    nb = n_slots // tm
    xmap = lambda b, j, be, nl: (jnp.minimum(b, nl[0] - 1), 0)
    grid_spec = pltpu.PrefetchScalarGridSpec(
        num_scalar_prefetch=2,
        grid=(nb, nj),
        in_specs=[pl.BlockSpec((tm, d), xmap),
                  pl.BlockSpec((1, d, th), lambda b, j, be, nl: (be[b], 0, j)),
                  pl.BlockSpec((1, d, th), lambda b, j, be, nl: (be[b], 0, nj + j)),
                  pl.BlockSpec((1, th, d), lambda b, j, be, nl: (be[b], j, 0)),
                  pl.BlockSpec((1, 1, th), lambda b, j, be, nl: (be[b], 0, j)),
                  pl.BlockSpec((1, 1, th), lambda b, j, be, nl: (be[b], 0, nj + j)),
                  pl.BlockSpec((1, 1, d), lambda b, j, be, nl: (be[b], 0, 0)),
                  pl.BlockSpec((tm, 1), lambda b, j, be, nl: (b, 0))],
        out_specs=pl.BlockSpec((tm, d), lambda b, j, be, nl: (b, 0)),
        scratch_shapes=[pltpu.VMEM((tm, d), f32)],
    )
    return pl.pallas_call(
        functools.partial(_ffn_kernel, nj=nj),
        grid_spec=grid_spec,
        out_shape=jax.ShapeDtypeStruct((n_slots, d), bf16),
        compiler_params=_cparams(("arbitrary", "arbitrary")),
        name="expert_ffn",
    )(block_expert, n_live, xs, w_gu_b, w_gu_b, w_down_b,
      b_gu_l.reshape(n_exp, 1, two_f), b_gu_l.reshape(n_exp, 1, two_f),
      b_down_l.reshape(n_exp, 1, d), slot_gate.reshape(n_slots, 1))


def _moe_dispatch(top_idx, gates, n_tok, n_experts):
    tm = MOE_TM
    n_assign = n_tok * TOP_EXPERTS
    flat_e = top_idx.reshape(-1)
    onehot = (flat_e[:, None] == jnp.arange(n_experts, dtype=jnp.int32)[None, :]).astype(jnp.int32)
    before = jnp.cumsum(onehot, axis=0) - onehot
    rank = jnp.sum(before * onehot, axis=1)
    counts = jnp.sum(onehot, axis=0)
    padded = ((counts + tm - 1) // tm) * tm
    pend = jnp.cumsum(padded)
    pstart = pend - padded
    dest = pstart[flat_e] + rank
    n_blocks = -(-n_assign // tm) + n_experts
    n_slots = n_blocks * tm
    flat_tok = jnp.arange(n_assign, dtype=jnp.int32) // TOP_EXPERTS
    slot_tok = jnp.full((n_slots,), n_tok, jnp.int32).at[dest].set(flat_tok)
    slot_gate = jnp.zeros((n_slots,), f32).at[dest].set(gates.reshape(-1))
    n_live = (pend[-1] // tm).astype(jnp.int32)
    blk_start = jnp.minimum(jnp.arange(n_blocks, dtype=jnp.int32), n_live - 1) * tm
    block_expert = jnp.searchsorted(pend, blk_start, side='right').astype(jnp.int32)
    block_expert = jnp.minimum(block_expert, n_experts - 1)
    return dest, slot_tok, slot_gate, block_expert, n_live.reshape(1)


def _final_kernel(x_ref, y_ref, gate_ref, g_ref, b_ref, o_ref, *, alpha):
    d = x_ref.shape[1]
    y = y_ref[:, 0:d].astype(f32)
    for k in range(1, TOP_EXPERTS):
        y = y + y_ref[:, k * d:(k + 1) * d].astype(f32)
    r = alpha * x_ref[...] + gate_ref[0] * y
    o_ref[...] = _layer_norm_rows(r, g_ref[...], b_ref[...])


def _final_ln(x2, y4, gate, ln_g, ln_b, alpha, seq):
    t, d = x2.shape
    tm = 256
    tpb = seq // tm
    return pl.pallas_call(
        functools.partial(_final_kernel, alpha=alpha),
        grid=(t // tm,),
        in_specs=[pl.BlockSpec((tm, d), lambda i: (i, 0)),
                  pl.BlockSpec((tm, TOP_EXPERTS * d), lambda i: (i, 0)),
                  pl.BlockSpec((1, 1, d), lambda i: (i // tpb, 0, 0)),
                  _full_spec((1, d)), _full_spec((1, d))],
        out_specs=pl.BlockSpec((tm, d), lambda i: (i, 0)),
        out_shape=jax.ShapeDtypeStruct((t, d), f32),
        compiler_params=_cparams(("parallel",)),
        name="moe_sum_ln",
    )(x2, y4, gate, ln_g.reshape(1, d), ln_b.reshape(1, d))


def kernel(x, c, positions, w_in, w_out, idx_ln_g, idx_ln_b, conv_w, conv_b, dt_bias, a_log,
           d_skip, ssm_norm_g, diff_lambda, diff_norm_g, ada_w, ada_b, ln_g, ln_b,
           router_w, router_b, w_gu, b_gu, w_down, b_down):
    bsz, seq, d = x.shape
    depth = w_in.shape[0]
    n_tok = bsz * seq
    n_experts = router_w.shape[-1]
    alpha = (2 * depth) ** 0.25

    mod = _ada_mod(c, ada_w, ada_b).reshape(depth, bsz, 6, 1, d)
    t128 = _rope_tables(positions, HEAD_DIM)
    t64 = _rope_tables(positions, IDX_DIM)
    x2 = x.reshape(n_tok, d)

    for layer in range(depth):
        shift_m, scale_m, gate_m, shift_f, scale_f, gate_f = (mod[layer, :, k] for k in range(6))
        w_dsa, w_ssm, w_diff, lng_p, lnb_p = _split_w_in(w_in[layer], idx_ln_g[layer], idx_ln_b[layer])

        qkv_a, iw = _proj_dsa(x2, scale_m, shift_m, w_dsa, t128, t64, lng_p, lnb_p, seq)
        out_a = _dsa_attention(qkv_a, iw, bsz, seq)
        ssm = _proj_ssm(x2, scale_m, shift_m, w_ssm, seq)
        out_b = _ssd_mixer(ssm, conv_w[layer], conv_b[layer], dt_bias[layer], a_log[layer],
                           d_skip[layer], ssm_norm_g[layer], bsz, seq)
        qkv_c = _proj_diff(x2, scale_m, shift_m, w_diff, t64, seq)
        lam_init = 0.8 - 0.6 * math.exp(-0.3 * layer)
        out_c = _diff_attention(qkv_c, diff_lambda[layer], diff_norm_g[layer], lam_init, bsz, seq)
        x2 = _out_proj(out_a, out_b, out_c, w_out[layer].astype(bf16), x2, gate_m,
                       ln_g[layer, 0], ln_b[layer, 0], alpha, seq)

        hf, top_idx, gates = _router(x2, scale_f, shift_f, router_w[layer], router_b[layer], seq)
        top_idx = top_idx[:, :TOP_EXPERTS]
        gates = gates[:, :TOP_EXPERTS]
        dest, slot_tok, slot_gate, block_expert, n_live = _moe_dispatch(top_idx, gates, n_tok, n_experts)
        hf_pad = jnp.concatenate([hf, jnp.zeros((1, d), hf.dtype)], axis=0)
        xs = hf_pad[slot_tok]
        ys = _expert_ffn(xs, block_expert, n_live, w_gu[layer].astype(bf16), b_gu[layer],
                         w_down[layer].astype(bf16), b_down[layer], slot_gate)
        y4 = ys[dest].reshape(n_tok, TOP_EXPERTS * d)
        x2 = _final_ln(x2, y4, gate_f, ln_g[layer, 1], ln_b[layer, 1], alpha, seq)

    return x2.reshape(bsz, seq, d)
```

```python
import functools
import math

import jax
import jax.numpy as jnp
from jax import lax
from jax.experimental import pallas as pl
from jax.experimental.pallas import tpu as pltpu

HEAD_DIM = 128
A_HEADS = 4
A_WIDTH = A_HEADS * HEAD_DIM
IDX_HEADS = 8
IDX_DIM = 64
TOPK_MAX = 256
SSM_HEADS = 16
SSM_HEAD_DIM = 64
SSM_WIDTH = SSM_HEADS * SSM_HEAD_DIM
SSM_GROUPS = 2
SSM_HEADS_PER_GROUP = SSM_HEADS // SSM_GROUPS
SSM_STATE = 128
CONV_WIDTH = 4
CONV_CH = SSM_WIDTH + 2 * SSM_GROUPS * SSM_STATE
SSD_CHUNK = 256
DIFF_HEADS = 4
DIFF_DIM = 64
DIFF_WIDTH = DIFF_HEADS * 2 * DIFF_DIM
TOP_EXPERTS = 4
SWIGLU_LIMIT = 7.0
SWIGLU_ALPHA = 1.702
ROPE_THETA = 500000.0
LN_EPS = 1e-5
RMS_EPS = 1e-6

LANES = 128
VMEM_LIMIT = 56 * 1024 * 1024
NEG_BIG = -1e30
LOG2E = 1.4426950408889634
KEY_FLOOR = -(2 ** 31) + 0x7FFFFF + 1
SEARCH_MAX_ITERS = 100

DSA_IN = 11 * LANES
DSA_OUT = 12 * LANES
SSM_COLS = SSM_WIDTH + CONV_CH + SSM_HEADS
SSM_PAD = 21 * LANES
DIFF_COLS = 3 * DIFF_WIDTH

bf16 = jnp.bfloat16
f32 = jnp.float32


def _cparams(sem, vmem=VMEM_LIMIT):
    return pltpu.CompilerParams(dimension_semantics=sem, vmem_limit_bytes=vmem)


def _split3(a):
    a1 = a.astype(bf16)
    r = a - a1.astype(f32)
    a2 = r.astype(bf16)
    a3 = (r - a2.astype(f32)).astype(bf16)
    return a1, a2, a3


def _dot(a, b):
    return jnp.dot(a, b, preferred_element_type=f32)


def _dot_nt(a, b):
    return lax.dot_general(a, b, (((1,), (1,)), ((), ())), preferred_element_type=f32)


def _dot_tn(a, b):
    return lax.dot_general(a, b, (((0,), (0,)), ((), ())), preferred_element_type=f32)


def _dot_f32(a, b3):
    a1, a2, a3 = _split3(a)
    b1, b2, b3_ = b3
    return (_dot(a1, b1) + (_dot(a1, b2) + _dot(a2, b1))
            + (_dot(a2, b2) + _dot(a1, b3_) + _dot(a3, b1)))


def _ada_kernel(c_ref, w_ref, b_ref, o_ref):
    c = c_ref[...]
    c = c * (1.0 / (1.0 + jnp.exp(-c)))
    w1, w2, w3 = _split3(w_ref[0])
    o_ref[0] = _dot_f32(c, (w1, w2, w3)) + b_ref[0]


def _ada_mod(c, ada_w, ada_b):
    depth, d, n = ada_w.shape
    bsz = c.shape[0]
    rows = 8
    cp = jnp.zeros((rows, d), f32).at[:bsz].set(c)
    tn = 512
    out = pl.pallas_call(
        _ada_kernel,
        grid=(depth, n // tn),
        in_specs=[pl.BlockSpec((rows, d), lambda l, j: (0, 0)),
                  pl.BlockSpec((1, d, tn), lambda l, j: (l, 0, j)),
                  pl.BlockSpec((1, 1, tn), lambda l, j: (l, 0, j))],
        out_specs=pl.BlockSpec((1, rows, tn), lambda l, j: (l, 0, j)),
        out_shape=jax.ShapeDtypeStruct((depth, rows, n), f32),
        compiler_params=_cparams(("parallel", "parallel")),
        name="ada_mod",
    )(cp, ada_w, ada_b.reshape(depth, 1, n))
    return out[:, :bsz]


def _rope_tables(positions, head_dim):
    rot = head_dim // 4
    half = rot // 2
    pos = positions.reshape(-1).astype(f32)
    inv_freq = ROPE_THETA ** (-jnp.arange(half, dtype=f32) / half)
    ang = pos[:, None] * inv_freq[None, :]
    cos, sin = jnp.cos(ang), jnp.sin(ang)
    lane = jnp.arange(LANES)
    q = lane % head_dim
    j = q % half
    in_rot = q < rot
    first = q < half
    c = jnp.where(in_rot[None, :], cos[:, j], 1.0)
    s1 = jnp.where(first[None, :], -sin[:, j], 0.0)
    s2 = jnp.where((in_rot & ~first)[None, :], sin[:, j], 0.0)
    return c.astype(f32), s1.astype(f32), s2.astype(f32)


def _rot_block(blk, c, s1, s2, half):
    return (blk * c + pltpu.roll(blk, LANES - half, 1) * s1
            + pltpu.roll(blk, half, 1) * s2)


def _modulate(x_ref, sc_ref, sh_ref):
    return (x_ref[...] * (1.0 + sc_ref[0]) + sh_ref[0]).astype(bf16)


def _proj_dsa_kernel(x_ref, sc_ref, sh_ref, w_ref, c128, s1_128, s2_128, c64, s1_64, s2_64,
                     lng_ref, lnb_ref, o_ref, iw_ref):
    h = _modulate(x_ref, sc_ref, sh_ref)
    acc = _dot(h, w_ref[...])
    t128 = (c128[...], s1_128[...], s2_128[...])
    t64 = (c64[...], s1_64[...], s2_64[...])
    a_scale = HEAD_DIM ** -0.5 * LOG2E
    for cb in range(4):
        blk = acc[:, cb * LANES:(cb + 1) * LANES]
        o_ref[:, cb * LANES:(cb + 1) * LANES] = (_rot_block(blk, *t128, 16) * a_scale).astype(bf16)
    for cb in range(4, 8):
        blk = acc[:, cb * LANES:(cb + 1) * LANES]
        o_ref[:, cb * LANES:(cb + 1) * LANES] = _rot_block(blk, *t64, 8).astype(bf16)
    blk = acc[:, 8 * LANES:9 * LANES]
    o_ref[:, 8 * LANES:9 * LANES] = _rot_block(blk, *t128, 16).astype(bf16)
    o_ref[:, 9 * LANES:10 * LANES] = acc[:, 9 * LANES:10 * LANES].astype(bf16)
    blk = acc[:, 10 * LANES:11 * LANES]
    lane = lax.broadcasted_iota(jnp.int32, blk.shape, 1)
    is_k = lane < IDX_DIM
    mu = jnp.sum(jnp.where(is_k, blk, 0.0), axis=1, keepdims=True) * (1.0 / IDX_DIM)
    d = jnp.where(is_k, blk - mu, 0.0)
    var = jnp.sum(d * d, axis=1, keepdims=True) * (1.0 / IDX_DIM)
    kn = d * lax.rsqrt(var + LN_EPS) * lng_ref[...] + lnb_ref[...]
    kn = jnp.where(is_k, _rot_block(kn, *t64, 8), 0.0)
    o_ref[:, 10 * LANES:11 * LANES] = kn.astype(bf16)
    o_ref[:, 11 * LANES:12 * LANES] = pltpu.roll(kn, IDX_DIM, 1).astype(bf16)
    iw_ref[...] = pltpu.roll(blk, IDX_DIM, 1) * (IDX_HEADS ** -0.5 * IDX_DIM ** -0.5)


def _proj_ssm_kernel(x_ref, sc_ref, sh_ref, w_ref, o_ref):
    h = _modulate(x_ref, sc_ref, sh_ref)
    o_ref[...] = _dot(h, w_ref[...])


def _proj_diff_kernel(x_ref, sc_ref, sh_ref, w_ref, c64, s1_64, s2_64, o_ref):
    h = _modulate(x_ref, sc_ref, sh_ref)
    acc = _dot(h, w_ref[...])
    t64 = (c64[...], s1_64[...], s2_64[...])
    q_scale = DIFF_DIM ** -0.5 * LOG2E
    for cb in range(4):
        blk = acc[:, cb * LANES:(cb + 1) * LANES]
        o_ref[:, cb * LANES:(cb + 1) * LANES] = (_rot_block(blk, *t64, 8) * q_scale).astype(bf16)
    for cb in range(4, 8):
        blk = acc[:, cb * LANES:(cb + 1) * LANES]
        o_ref[:, cb * LANES:(cb + 1) * LANES] = _rot_block(blk, *t64, 8).astype(bf16)
    o_ref[:, 8 * LANES:] = acc[:, 8 * LANES:].astype(bf16)


def _row_specs(tm, d, tiles_per_batch):
    return [pl.BlockSpec((tm, d), lambda i: (i, 0)),
            pl.BlockSpec((1, 1, d), lambda i: (i // tiles_per_batch, 0, 0)),
            pl.BlockSpec((1, 1, d), lambda i: (i // tiles_per_batch, 0, 0))]


def _tab_spec(tm):
    return pl.BlockSpec((tm, LANES), lambda i: (i, 0))


def _full_spec(shape):
    nd = len(shape)
    return pl.BlockSpec(shape, lambda i: (0,) * nd)


def _proj_dsa(x2, scale, shift, w, t128, t64, lng, lnb, seq):
    t, d = x2.shape
    tm = 256
    return pl.pallas_call(
        _proj_dsa_kernel,
        grid=(t // tm,),
        in_specs=_row_specs(tm, d, seq // tm) + [_full_spec(w.shape)] + [_tab_spec(tm)] * 6
        + [_full_spec((1, LANES))] * 2,
        out_specs=[pl.BlockSpec((tm, DSA_OUT), lambda i: (i, 0)),
                   pl.BlockSpec((tm, LANES), lambda i: (i, 0))],
        out_shape=[jax.ShapeDtypeStruct((t, DSA_OUT), bf16),
                   jax.ShapeDtypeStruct((t, LANES), f32)],
        compiler_params=_cparams(("parallel",)),
        name="proj_dsa",
    )(x2, scale, shift, w, *t128, *t64, lng, lnb)


def _proj_ssm(x2, scale, shift, w, seq):
    t, d = x2.shape
    tm = 256
    return pl.pallas_call(
        _proj_ssm_kernel,
        grid=(t // tm,),
        in_specs=_row_specs(tm, d, seq // tm) + [_full_spec(w.shape)],
        out_specs=pl.BlockSpec((tm, SSM_PAD), lambda i: (i, 0)),
        out_shape=jax.ShapeDtypeStruct((t, SSM_PAD), f32),
        compiler_params=_cparams(("parallel",)),
        name="proj_ssm",
    )(x2, scale, shift, w)


def _proj_diff(x2, scale, shift, w, t64, seq):
    t, d = x2.shape
    tm = 256
    return pl.pallas_call(
        _proj_diff_kernel,
        grid=(t // tm,),
        in_specs=_row_specs(tm, d, seq // tm) + [_full_spec(w.shape)] + [_tab_spec(tm)] * 3,
        out_specs=pl.BlockSpec((tm, DIFF_COLS), lambda i: (i, 0)),
        out_shape=jax.ShapeDtypeStruct((t, DIFF_COLS), bf16),
        compiler_params=_cparams(("parallel",)),
        name="proj_diff",
    )(x2, scale, shift, w, *t64)


def _split_w_in(w_in_l, lng, lnb):
    d = w_in_l.shape[0]
    sizes = (A_WIDTH, HEAD_DIM, HEAD_DIM, IDX_HEADS * IDX_DIM, IDX_DIM, IDX_HEADS,
             SSM_WIDTH, CONV_CH, SSM_HEADS, DIFF_WIDTH, DIFF_WIDTH, DIFF_WIDTH)
    offs = [0]
    for s in sizes:
        offs.append(offs[-1] + s)
    col = lambda k: w_in_l[:, offs[k]:offs[k + 1]]
    pad = lambda n: jnp.zeros((d, n), w_in_l.dtype)
    w_dsa = jnp.concatenate([col(0), col(3), col(1), col(2), col(4), col(5),
                             pad(LANES - IDX_DIM - IDX_HEADS)], axis=1).astype(bf16)
    w_ssm = jnp.concatenate([col(6), col(7), col(8), pad(SSM_PAD - SSM_COLS)], axis=1).astype(bf16)
    w_diff = jnp.concatenate([col(9), col(10), col(11)], axis=1).astype(bf16)
    lng_p = jnp.zeros((1, LANES), f32).at[0, :IDX_DIM].set(lng)
    lnb_p = jnp.zeros((1, LANES), f32).at[0, :IDX_DIM].set(lnb)
    return w_dsa, w_ssm, w_diff, lng_p, lnb_p


def _sortable(x):
    b = lax.bitcast_convert_type(x, jnp.int32)
    return b ^ ((b >> 31) & jnp.int32(0x7FFFFFFF))


def _online_softmax_step(s, v_blk, carry):
    m, l, acc = carry
    m_new = jnp.maximum(m, jnp.max(s, axis=1, keepdims=True))
    alpha = jnp.exp2(m - m_new)
    p = jnp.exp2(s - m_new)
    l = alpha * l + jnp.sum(p, axis=1, keepdims=True)
    acc = alpha * acc + _dot(p.astype(bf16), v_blk)
    return m_new, l, acc


def _flash_pipelined(logits_fn, v_ref, n_chunks, tk, rows):
    def process(s, idx, carry):
        off = pl.multiple_of(idx * tk, tk)
        return _online_softmax_step(s, v_ref[pl.ds(off, tk), :], carry)

    init = (jnp.full((rows, 1), NEG_BIG, f32), jnp.zeros((rows, 1), f32),
            jnp.zeros((rows, LANES), f32))
    last = n_chunks - 1
    carry = process(logits_fn(last, True), last, init)

    def pair(t, carry):
        s_a = logits_fn(2 * t, False)
        s_b = logits_fn(2 * t + 1, False)
        return process(s_b, 2 * t + 1, process(s_a, 2 * t, carry))

    def single(kc, carry):
        return process(logits_fn(kc, False), kc, carry)

    n_pairs = last // 2
    carry = lax.fori_loop(0, n_pairs, pair, carry)
    _, l, acc = lax.fori_loop(2 * n_pairs, last, single, carry)
    return acc / l


def _dsa_kernel(aq_ref, iq_ref, ak_ref, av_ref, ika_ref, ikb_ref, iw_ref, o_ref, keys_ref,
                *, tq, tk, topk):
    i = pl.program_id(1)
    q_start = i * tq
    n_chunks = (q_start + tq + tk - 1) // tk
    row = q_start + lax.broadcasted_iota(jnp.int32, (tq, tk), 0)
    iw = iw_ref[...]
    w_cols = [iw[:, h:h + 1] for h in range(IDX_HEADS)]

    def score_chunk(kc, pmax):
        off = pl.multiple_of(kc * tk, tk)
        ka = ika_ref[pl.ds(off, tk), :]
        kb = ikb_ref[pl.ds(off, tk), :]
        sc = jnp.zeros((tq, tk), f32)
        for m in range(IDX_HEADS // 2):
            qp = iq_ref[:, m * LANES:(m + 1) * LANES]
            sc = sc + w_cols[2 * m] * jnp.maximum(_dot_nt(qp, ka), 0.0)
            sc = sc + w_cols[2 * m + 1] * jnp.maximum(_dot_nt(qp, kb), 0.0)
        col = off + lax.broadcasted_iota(jnp.int32, (tq, tk), 1)
        sc = jnp.where(col <= row, sc, -jnp.inf)
        keys_ref[:, pl.ds(off, tk)] = _sortable(sc)
        for c in range(tk // LANES):
            pmax = jnp.maximum(pmax, sc[:, c * LANES:(c + 1) * LANES])
        return pmax

    pmax = lax.fori_loop(0, n_chunks, score_chunk, jnp.full((tq, LANES), -jnp.inf, f32))
    kmax = _sortable(jnp.max(pmax, axis=1, keepdims=True))

    def count_ge(cand):
        def body(kc, part):
            off = pl.multiple_of(kc * tk, tk)
            hit = jnp.where(keys_ref[:, pl.ds(off, tk)] >= cand, 1.0, 0.0)
            for c in range(tk // LANES):
                part = part + hit[:, c * LANES:(c + 1) * LANES]
            return part
        part = lax.fori_loop(0, n_chunks, body, jnp.zeros((tq, LANES), f32))
        return jnp.sum(part, axis=1, keepdims=True)

    kf = float(topk)
    zero = jnp.zeros((tq, 1), jnp.int32)
    n_real = (row[:, 0:1] + 1).astype(f32)
    c0 = count_ge(zero)
    pos = c0 >= kf
    lo0 = jnp.where(pos, zero, zero + KEY_FLOOR)
    hi0 = jnp.where(pos, kmax + 1, zero)
    clo0 = jnp.where(pos, c0, n_real)
    chi0 = jnp.where(pos, 0.0, c0)

    def active_rows(lo, hi, c_lo):
        return jnp.logical_and(c_lo > kf, hi - lo > 1)

    def search_cond(st):
        it, lo, hi, c_lo, _ = st
        live = jnp.max(jnp.where(active_rows(lo, hi, c_lo), 1.0, 0.0))
        return jnp.logical_and(it < SEARCH_MAX_ITERS, live > 0.0)

    def search_body(st):
        it, lo, hi, c_lo, c_hi = st
        act = active_rows(lo, hi, c_lo)
        width = hi - lo
        frac = (c_lo - kf) / jnp.maximum(c_lo - c_hi, 1.0)
        frac = jnp.where(it % 3 == 2, 0.5, frac)
        step = jnp.minimum(width.astype(f32) * frac, 2147483520.0).astype(jnp.int32)
        cand = lo + jnp.clip(step, 1, jnp.maximum(width - 1, 1))
        c = count_ge(cand)
        up = jnp.logical_and(act, c >= kf)
        dn = jnp.logical_and(act, c < kf)
        return (it + 1, jnp.where(up, cand, lo), jnp.where(dn, cand, hi),
                jnp.where(up, c, c_lo), jnp.where(dn, c, c_hi))

    _, thr, _, _, _ = lax.while_loop(search_cond, search_body,
                                     (jnp.int32(0), lo0, hi0, clo0, chi0))

    q4 = jnp.concatenate([aq_ref[:, h * LANES:(h + 1) * LANES] for h in range(A_HEADS)], axis=0)

    def logits(kc, diagonal):
        del diagonal
        off = pl.multiple_of(kc * tk, tk)
        bias = jnp.where(keys_ref[:, pl.ds(off, tk)] >= thr, 0.0, NEG_BIG)
        return _dot_nt(q4, ak_ref[pl.ds(off, tk), :]) + jnp.concatenate([bias] * A_HEADS, axis=0)

    o = _flash_pipelined(logits, av_ref, n_chunks, tk, A_HEADS * tq)
    for h in range(A_HEADS):
        o_ref[:, h * LANES:(h + 1) * LANES] = o[h * tq:(h + 1) * tq].astype(bf16)


def _dsa_attention(qkv, iw, bsz, seq):
    tq = min(128, seq)
    tk = min(512, seq)
    nq = seq // tq
    topk = min(TOPK_MAX, seq // 4)
    qspec = lambda c: pl.BlockSpec((tq, A_WIDTH), lambda b, i: (b * nq + i, c))
    kspec = lambda c: pl.BlockSpec((seq, LANES), lambda b, i: (b, c))
    return pl.pallas_call(
        functools.partial(_dsa_kernel, tq=tq, tk=tk, topk=topk),
        grid=(bsz, nq),
        in_specs=[qspec(0), qspec(1), kspec(8), kspec(9), kspec(10), kspec(11),
                  pl.BlockSpec((tq, LANES), lambda b, i: (b * nq + i, 0))],
        out_specs=pl.BlockSpec((tq, A_WIDTH), lambda b, i: (b * nq + i, 0)),
        out_shape=jax.ShapeDtypeStruct((bsz * seq, A_WIDTH), bf16),
        scratch_shapes=[pltpu.VMEM((tq, seq), jnp.int32)],
        compiler_params=_cparams(("parallel", "arbitrary")),
        name="dsa_attention",
    )(qkv, qkv, qkv, qkv, qkv, qkv, iw)


def _diff_kernel(q_ref, k_ref, v_ref, lam_ref, g_ref, o_ref, *, tq, tk, lam_init):
    i = pl.program_id(2)
    q_start = i * tq
    n_chunks = (q_start + tq + tk - 1) // tk
    qf = q_ref[...].astype(f32)
    lane = lax.broadcasted_iota(jnp.int32, qf.shape, 1)
    q2 = jnp.concatenate([jnp.where(lane < DIFF_DIM, qf, 0.0),
                          jnp.where(lane >= DIFF_DIM, qf, 0.0)], axis=0).astype(bf16)

    def logits(kc, diagonal):
        off = pl.multiple_of(kc * tk, tk)
        s = _dot_nt(q2, k_ref[pl.ds(off, tk), :])
        if diagonal:
            row = q_start + lax.broadcasted_iota(jnp.int32, (tq, tk), 0)
            col = off + lax.broadcasted_iota(jnp.int32, (2 * tq, tk), 1)
            s = jnp.where(col <= jnp.concatenate([row, row], axis=0), s, NEG_BIG)
        return s

    o = _flash_pipelined(logits, v_ref, n_chunks, tk, 2 * tq)
    lp = lam_ref[...]
    lam = (jnp.exp(jnp.sum(lp[0:1] * lp[1:2], axis=1, keepdims=True))
           - jnp.exp(jnp.sum(lp[2:3] * lp[3:4], axis=1, keepdims=True)) + lam_init)
    o = o[:tq] - lam * o[tq:]
    ms = jnp.mean(o * o, axis=1, keepdims=True)
    o_ref[...] = (o * lax.rsqrt(ms + RMS_EPS) * g_ref[...] * (1.0 - lam_init)).astype(bf16)


def _diff_attention(qkv, diff_lambda_l, diff_norm_g_l, lam_init, bsz, seq):
    tq = min(256, seq)
    tk = min(512, seq)
    nq = seq // tq
    return pl.pallas_call(
        functools.partial(_diff_kernel, tq=tq, tk=tk, lam_init=lam_init),
        grid=(bsz, DIFF_HEADS, nq),
        in_specs=[pl.BlockSpec((tq, LANES), lambda b, h, i: (b * nq + i, h)),
                  pl.BlockSpec((seq, LANES), lambda b, h, i: (b, DIFF_HEADS + h)),
                  pl.BlockSpec((seq, LANES), lambda b, h, i: (b, 2 * DIFF_HEADS + h)),
                  pl.BlockSpec((4, DIFF_DIM), lambda b, h, i: (0, 0)),
                  pl.BlockSpec((1, LANES), lambda b, h, i: (0, 0))],
        out_specs=pl.BlockSpec((tq, LANES), lambda b, h, i: (b * nq + i, h)),
        out_shape=jax.ShapeDtypeStruct((bsz * seq, DIFF_WIDTH), bf16),
        compiler_params=_cparams(("parallel", "parallel", "arbitrary")),
        name="diff_attention",
    )(qkv, qkv, qkv, diff_lambda_l, diff_norm_g_l.reshape(1, LANES))


def _sigmoid(x):
    return 1.0 / (1.0 + jnp.exp(-x))


def _expand_heads(v, e_ref):
    v1, v2, v3 = _split3(v)
    e = e_ref[...]
    return _dot(v1, e) + _dot(v2, e) + _dot(v3, e)


def _ssd_kernel(s_ref, cw_ref, cb_ref, dtb_ref, alog_ref, dskip_ref, ng_ref, e_ref, o_ref,
                xpad_ref, state_ref, *, q):
    hp = SSM_HEADS_PER_GROUP * SSM_HEAD_DIM
    xo, do = SSM_WIDTH, SSM_WIDTH + CONV_CH

    @pl.when(pl.program_id(1) == 0)
    def _():
        xpad_ref[0:8, :] = jnp.zeros((8, CONV_CH), f32)
        state_ref[...] = jnp.zeros(state_ref.shape, f32)

    xpad_ref[8:, :] = s_ref[:, xo:do]
    cw = cw_ref[...]
    y = cb_ref[...] + cw[3:4] * xpad_ref[8:q + 8, :]
    y = y + cw[2:3] * xpad_ref[7:q + 7, :]
    y = y + cw[1:2] * xpad_ref[6:q + 6, :]
    y = y + cw[0:1] * xpad_ref[5:q + 5, :]
    xpad_ref[0:8, :] = xpad_ref[q:q + 8, :]
    xa = y * _sigmoid(y)
    xs = xa[:, :SSM_WIDTH]

    dtr = s_ref[:, do:do + LANES] + dtb_ref[...]
    dt = jnp.maximum(dtr, 0.0) + jnp.log1p(jnp.exp(-jnp.abs(dtr)))
    a = dt * (-jnp.exp(alog_ref[...]))
    ri = lax.broadcasted_iota(jnp.int32, (q, q), 0)
    ci = lax.broadcasted_iota(jnp.int32, (q, q), 1)
    tril = ri >= ci
    tri = jnp.where(tril, 1.0, 0.0).astype(bf16)
    a1, a2, a3 = _split3(a)
    a_cum = _dot(tri, a1) + _dot(tri, a2) + _dot(tri, a3)
    a_cum_t = a_cum.T
    a_last = a_cum[q - 1:q, :]
    dt_x = _expand_heads(dt, e_ref)
    ea_x = _expand_heads(jnp.exp(a_cum), e_ref)
    dte_x = _expand_heads(jnp.exp(a_last - a_cum), e_ref)
    xdt = xs * dt_x
    xw = (xdt * dte_x).astype(bf16)
    lane = lax.broadcasted_iota(jnp.int32, (q, LANES), 1)
    lo_half = lane < SSM_HEAD_DIM

    outs = []
    for g in range(SSM_GROUPS):
        bg = xa[:, SSM_WIDTH + g * SSM_STATE:SSM_WIDTH + (g + 1) * SSM_STATE].astype(bf16)
        cg = xa[:, SSM_WIDTH + (SSM_GROUPS + g) * SSM_STATE:
                SSM_WIDTH + (SSM_GROUPS + g + 1) * SSM_STATE].astype(bf16)
        cbm = _dot_nt(cg, bg)
        pieces = []
        for pr in range(SSM_HEADS_PER_GROUP // 2):
            c0 = g * hp + pr * LANES
            xpair = xdt[:, c0:c0 + LANES]
            acc = None
            for sub in range(2):
                h = g * SSM_HEADS_PER_GROUP + 2 * pr + sub
                seg = a_cum[:, h:h + 1] - a_cum_t[h:h + 1, :]
                dec = jnp.where(tril, jnp.exp(seg), 0.0)
                mm = (cbm * dec).astype(bf16)
                keep = lo_half if sub == 0 else jnp.logical_not(lo_half)
                part = _dot(mm, jnp.where(keep, xpair, 0.0).astype(bf16))
                acc = part if acc is None else acc + part
            pieces.append(acc)
        y_diag = jnp.concatenate(pieces, axis=1)
        prev = state_ref[g]
        y_off = _dot(cg, prev.astype(bf16)) * ea_x[:, g * hp:(g + 1) * hp]
        new_states = _dot_tn(bg, xw[:, g * hp:(g + 1) * hp])
        state_ref[g] = prev * ea_x[q - 1:q, g * hp:(g + 1) * hp] + new_states
        outs.append(y_diag + y_off)
    yv = jnp.concatenate(outs, axis=1) + dskip_ref[...] * xs
    z = s_ref[:, :SSM_WIDTH]
    yv = yv * (z * _sigmoid(z))
    ng = ng_ref[...]
    for g in range(SSM_GROUPS):
        yg = yv[:, g * hp:(g + 1) * hp]
        ms = jnp.mean(yg * yg, axis=1, keepdims=True)
        o_ref[:, g * hp:(g + 1) * hp] = (yg * lax.rsqrt(ms + RMS_EPS)
                                         * ng[:, g * hp:(g + 1) * hp]).astype(bf16)


def _ssd_mixer(ssm, conv_w, conv_b, dt_bias, a_log, d_skip, norm_g, bsz, seq):
    q = math.gcd(SSD_CHUNK, seq)
    nc = seq // q
    pad16 = lambda v: jnp.zeros((1, LANES), f32).at[0, :SSM_HEADS].set(v)
    head_of_lane = jnp.arange(SSM_WIDTH) // SSM_HEAD_DIM
    expand = (jnp.arange(LANES)[:, None] == head_of_lane[None, :]).astype(bf16)
    const = lambda shape: pl.BlockSpec(shape, lambda b, c: (0,) * len(shape))
    return pl.pallas_call(
        functools.partial(_ssd_kernel, q=q),
        grid=(bsz, nc),
        in_specs=[pl.BlockSpec((q, SSM_PAD), lambda b, c: (b * nc + c, 0)),
                  const((CONV_WIDTH, CONV_CH)), const((1, CONV_CH)), const((1, LANES)),
                  const((1, LANES)), const((1, SSM_WIDTH)), const((1, SSM_WIDTH)),
                  const((LANES, SSM_WIDTH))],
        out_specs=pl.BlockSpec((q, SSM_WIDTH), lambda b, c: (b * nc + c, 0)),
        out_shape=jax.ShapeDtypeStruct((bsz * seq, SSM_WIDTH), bf16),
        scratch_shapes=[pltpu.VMEM((q + 8, CONV_CH), f32),
                        pltpu.VMEM((SSM_GROUPS, SSM_STATE, SSM_HEADS_PER_GROUP * SSM_HEAD_DIM), f32)],
        compiler_params=_cparams(("parallel", "arbitrary")),
        name="ssd_mixer",
    )(ssm, conv_w, conv_b.reshape(1, CONV_CH), pad16(dt_bias), pad16(a_log),
      jnp.repeat(d_skip, SSM_HEAD_DIM).reshape(1, SSM_WIDTH), norm_g.reshape(1, SSM_WIDTH), expand)


def _layer_norm_rows(r, g, b):
    mu = jnp.mean(r, axis=1, keepdims=True)
    d = r - mu
    var = jnp.mean(d * d, axis=1, keepdims=True)
    return d * lax.rsqrt(var + LN_EPS) * g + b


def _outproj_kernel(oa_ref, ob_ref, oc_ref, w_ref, x_ref, gate_ref, g_ref, b_ref, o_ref, *, alpha):
    y = _dot(oa_ref[...], w_ref[0:A_WIDTH, :])
    y = y + _dot(ob_ref[...], w_ref[A_WIDTH:A_WIDTH + SSM_WIDTH, :])
    y = y + _dot(oc_ref[...], w_ref[A_WIDTH + SSM_WIDTH:, :])
    r = alpha * x_ref[...] + gate_ref[0] * y
    o_ref[...] = _layer_norm_rows(r, g_ref[...], b_ref[...])


def _out_proj(oa, ob, oc, w_out_b, x2, gate, ln_g, ln_b, alpha, seq):
    t, d = x2.shape
    tm = 256
    tpb = seq // tm
    rows = lambda w: pl.BlockSpec((tm, w), lambda i: (i, 0))
    return pl.pallas_call(
        functools.partial(_outproj_kernel, alpha=alpha),
        grid=(t // tm,),
        in_specs=[rows(A_WIDTH), rows(SSM_WIDTH), rows(DIFF_WIDTH), _full_spec(w_out_b.shape),
                  rows(d), pl.BlockSpec((1, 1, d), lambda i: (i // tpb, 0, 0)),
                  _full_spec((1, d)), _full_spec((1, d))],
        out_specs=rows(d),
        out_shape=jax.ShapeDtypeStruct((t, d), f32),
        compiler_params=_cparams(("parallel",)),
        name="out_proj_ln",
    )(oa, ob, oc, w_out_b, x2, gate, ln_g.reshape(1, d), ln_b.reshape(1, d))


def _router_kernel(x_ref, sc_ref, sh_ref, w1_ref, w2_ref, w3_ref, b_ref, h_ref, idx_ref, gate_ref,
                   *, n_experts):
    h = x_ref[...] * (1.0 + sc_ref[0]) + sh_ref[0]
    h_ref[...] = h.astype(bf16)
    logits = _dot_f32(h, (w1_ref[...], w2_ref[...], w3_ref[...])) + b_ref[...]
    lane = lax.broadcasted_iota(jnp.int32, logits.shape, 1)
    lane_f = lane.astype(f32)
    cur = jnp.where(lane < n_experts, logits, -jnp.inf)
    vals, idxs = [], []
    for _ in range(TOP_EXPERTS):
        m = jnp.max(cur, axis=1, keepdims=True)
        ix = jnp.min(jnp.where(cur == m, lane_f, float(LANES)), axis=1, keepdims=True)
        vals.append(m)
        idxs.append(ix)
        cur = jnp.where(lane_f == ix, -jnp.inf, cur)
    es = [jnp.exp(v - vals[0]) for v in vals]
    tot = es[0] + es[1] + es[2] + es[3]
    gates = jnp.zeros(logits.shape, f32)
    idx_o = jnp.zeros(logits.shape, f32)
    for k in range(TOP_EXPERTS):
        gates = jnp.where(lane == k, es[k] / tot, gates)
        idx_o = jnp.where(lane == k, idxs[k], idx_o)
    gate_ref[...] = gates
    idx_ref[...] = idx_o.astype(jnp.int32)


def _router(x2, scale, shift, router_w_l, router_b_l, seq):
    t, d = x2.shape
    n_experts = router_w_l.shape[1]
    tm = 256
    wp = jnp.zeros((d, LANES), f32).at[:, :n_experts].set(router_w_l)
    w1, w2, w3 = _split3(wp)
    bp = jnp.zeros((1, LANES), f32).at[0, :n_experts].set(router_b_l)
    rows = lambda w: pl.BlockSpec((tm, w), lambda i: (i, 0))
    return pl.pallas_call(
        functools.partial(_router_kernel, n_experts=n_experts),
        grid=(t // tm,),
        in_specs=_row_specs(tm, d, seq // tm) + [_full_spec((d, LANES))] * 3 + [_full_spec((1, LANES))],
        out_specs=[rows(d), rows(LANES), rows(LANES)],
        out_shape=[jax.ShapeDtypeStruct((t, d), bf16),
                   jax.ShapeDtypeStruct((t, LANES), jnp.int32),
                   jax.ShapeDtypeStruct((t, LANES), f32)],
        compiler_params=_cparams(("parallel",)),
        name="router",
    )(x2, scale, shift, w1, w2, w3, bp)


MOE_TM = 1024
MOE_TH = 256


def _ffn_kernel(be_ref, nb_ref, x_ref, wg_ref, wu_ref, wd_ref, bg_ref, bu_ref, bd_ref,
                o_ref, acc_ref, *, nj):
    blk = pl.program_id(0)
    j = pl.program_id(1)
    live = blk < nb_ref[0]

    @pl.when(live)
    def _():
        x = x_ref[...]
        g = _dot(x, wg_ref[0].astype(bf16)) + bg_ref[0]
        u = _dot(x, wu_ref[0].astype(bf16)) + bu_ref[0]
        g = jnp.minimum(g, SWIGLU_LIMIT)
        u = jnp.clip(u, -SWIGLU_LIMIT, SWIGLU_LIMIT)
        act = (g * _sigmoid(SWIGLU_ALPHA * g) * (u + 1.0)).astype(bf16)
        part = _dot(act, wd_ref[0].astype(bf16))

        @pl.when(j == 0)
        def _():
            acc_ref[...] = part

        @pl.when(j > 0)
        def _():
            acc_ref[...] += part

    @pl.when(j == nj - 1)
    def _():
        y = acc_ref[...] + bd_ref[0]
        o_ref[...] = jnp.where(live, y, 0.0).astype(bf16)


def _expert_ffn(xs, block_expert, n_live, w_gu, b_gu, w_down, b_down, layer):
    n_slots, d = xs.shape
    depth, n_exp, _, two_f = w_gu.shape
    fdim = two_f // 2
    tm, th = MOE_TM, min(MOE_TH, fdim)
    nj = fdim // th
    nb = n_slots // tm
    xmap = lambda b, j, be, nl: (jnp.minimum(b, nl[0] - 1), 0)
    grid_spec = pltpu.PrefetchScalarGridSpec(
        num_scalar_prefetch=2,
        grid=(nb, nj),
        in_specs=[pl.BlockSpec((tm, d), xmap),
                  pl.BlockSpec((None, 1, d, th), lambda b, j, be, nl: (layer, be[b], 0, j)),
                  pl.BlockSpec((None, 1, d, th), lambda b, j, be, nl: (layer, be[b], 0, nj + j)),
                  pl.BlockSpec((None, 1, th, d), lambda b, j, be, nl: (layer, be[b], j, 0)),
                  pl.BlockSpec((None, 1, 1, th), lambda b, j, be, nl: (layer, be[b], 0, j)),
                  pl.BlockSpec((None, 1, 1, th), lambda b, j, be, nl: (layer, be[b], 0, nj + j)),
                  pl.BlockSpec((None, 1, 1, d), lambda b, j, be, nl: (layer, be[b], 0, 0))],
        out_specs=pl.BlockSpec((tm, d), lambda b, j, be, nl: (b, 0)),
        scratch_shapes=[pltpu.VMEM((tm, d), f32)],
    )
    b_gu4 = b_gu.reshape(depth, n_exp, 1, two_f)
    return pl.pallas_call(
        functools.partial(_ffn_kernel, nj=nj),
        grid_spec=grid_spec,
        out_shape=jax.ShapeDtypeStruct((n_slots, d), bf16),
        compiler_params=_cparams(("arbitrary", "arbitrary")),
        name="expert_ffn",
    )(block_expert, n_live, xs, w_gu, w_gu, w_down, b_gu4, b_gu4,
      b_down.reshape(depth, n_exp, 1, d))


def _moe_dispatch(top_idx, n_tok, n_experts):
    tm = MOE_TM
    n_assign = n_tok * TOP_EXPERTS
    flat_e = top_idx.reshape(-1)
    onehot = (flat_e[:, None] == jnp.arange(n_experts, dtype=jnp.int32)[None, :]).astype(jnp.int32)
    before = jnp.cumsum(onehot, axis=0) - onehot
    rank = jnp.sum(before * onehot, axis=1)
    counts = jnp.sum(onehot, axis=0)
    padded = ((counts + tm - 1) // tm) * tm
    pend = jnp.cumsum(padded)
    pstart = pend - padded
    dest = pstart[flat_e] + rank
    n_blocks = -(-n_assign // tm) + n_experts
    n_slots = n_blocks * tm
    flat_tok = jnp.arange(n_assign, dtype=jnp.int32) // TOP_EXPERTS
    slot_tok = jnp.full((n_slots,), n_tok, jnp.int32).at[dest].set(flat_tok)
    n_live = (pend[-1] // tm).astype(jnp.int32)
    blk_start = jnp.minimum(jnp.arange(n_blocks, dtype=jnp.int32), n_live - 1) * tm
    block_expert = jnp.searchsorted(pend, blk_start, side='right').astype(jnp.int32)
    block_expert = jnp.minimum(block_expert, n_experts - 1)
    return dest, slot_tok, block_expert, n_live.reshape(1)


def _final_kernel(x_ref, y_ref, rg_ref, gate_ref, g_ref, b_ref, o_ref, *, alpha):
    rg = rg_ref[...]
    y = rg[:, 0:1] * y_ref[0].astype(f32)
    for k in range(1, TOP_EXPERTS):
        y = y + rg[:, k:k + 1] * y_ref[k].astype(f32)
    r = alpha * x_ref[...] + gate_ref[0] * y
    o_ref[...] = _layer_norm_rows(r, g_ref[...], b_ref[...])


def _final_ln(x2, y4, rgates, gate, ln_g, ln_b, alpha, seq):
    t, d = x2.shape
    tm = 256
    tpb = seq // tm
    return pl.pallas_call(
        functools.partial(_final_kernel, alpha=alpha),
        grid=(t // tm,),
        in_specs=[pl.BlockSpec((tm, d), lambda i: (i, 0)),
                  pl.BlockSpec((TOP_EXPERTS, tm, d), lambda i: (0, i, 0)),
                  pl.BlockSpec((tm, LANES), lambda i: (i, 0)),
                  pl.BlockSpec((1, 1, d), lambda i: (i // tpb, 0, 0)),
                  _full_spec((1, d)), _full_spec((1, d))],
        out_specs=pl.BlockSpec((tm, d), lambda i: (i, 0)),
        out_shape=jax.ShapeDtypeStruct((t, d), f32),
        compiler_params=_cparams(("parallel",)),
        name="moe_sum_ln",
    )(x2, y4, rgates, gate, ln_g.reshape(1, d), ln_b.reshape(1, d))


def kernel(x, c, positions, w_in, w_out, idx_ln_g, idx_ln_b, conv_w, conv_b, dt_bias, a_log,
           d_skip, ssm_norm_g, diff_lambda, diff_norm_g, ada_w, ada_b, ln_g, ln_b,
           router_w, router_b, w_gu, b_gu, w_down, b_down):
    bsz, seq, d = x.shape
    depth = w_in.shape[0]
    n_tok = bsz * seq
    n_experts = router_w.shape[-1]
    alpha = (2 * depth) ** 0.25

    mod = _ada_mod(c, ada_w, ada_b).reshape(depth, bsz, 6, 1, d)
    t128 = _rope_tables(positions, HEAD_DIM)
    t64 = _rope_tables(positions, IDX_DIM)
    x2 = x.reshape(n_tok, d)

    for layer in range(depth):
        shift_m, scale_m, gate_m, shift_f, scale_f, gate_f = (mod[layer, :, k] for k in range(6))
        w_dsa, w_ssm, w_diff, lng_p, lnb_p = _split_w_in(w_in[layer], idx_ln_g[layer], idx_ln_b[layer])

        qkv_a, iw = _proj_dsa(x2, scale_m, shift_m, w_dsa, t128, t64, lng_p, lnb_p, seq)
        out_a = _dsa_attention(qkv_a, iw, bsz, seq)
        ssm = _proj_ssm(x2, scale_m, shift_m, w_ssm, seq)
        out_b = _ssd_mixer(ssm, conv_w[layer], conv_b[layer], dt_bias[layer], a_log[layer],
                           d_skip[layer], ssm_norm_g[layer], bsz, seq)
        qkv_c = _proj_diff(x2, scale_m, shift_m, w_diff, t64, seq)
        lam_init = 0.8 - 0.6 * math.exp(-0.3 * layer)
        out_c = _diff_attention(qkv_c, diff_lambda[layer], diff_norm_g[layer], lam_init, bsz, seq)
        x2 = _out_proj(out_a, out_b, out_c, w_out[layer].astype(bf16), x2, gate_m,
                       ln_g[layer, 0], ln_b[layer, 0], alpha, seq)

        hf, top_idx, gates = _router(x2, scale_f, shift_f, router_w[layer], router_b[layer], seq)
        dest, slot_tok, block_expert, n_live = _moe_dispatch(top_idx[:, :TOP_EXPERTS], n_tok, n_experts)
        xs = hf[jnp.minimum(slot_tok, n_tok - 1)]
        ys = _expert_ffn(xs, block_expert, n_live, w_gu, b_gu, w_down, b_down, layer)
        y4 = ys[dest.reshape(n_tok, TOP_EXPERTS).T]
        x2 = _final_ln(x2, y4, gates, gate_f, ln_g[layer, 1], ln_b[layer, 1], alpha, seq)

    return x2.reshape(bsz, seq, d)
```

```python
import functools
import math

import jax
import jax.numpy as jnp
from jax import lax
from jax.experimental import pallas as pl
from jax.experimental.pallas import tpu as pltpu

HEAD_DIM = 128
A_HEADS = 4
A_WIDTH = A_HEADS * HEAD_DIM
IDX_HEADS = 8
IDX_DIM = 64
TOPK_MAX = 256
SSM_HEADS = 16
SSM_HEAD_DIM = 64
SSM_WIDTH = SSM_HEADS * SSM_HEAD_DIM
SSM_GROUPS = 2
SSM_HEADS_PER_GROUP = SSM_HEADS // SSM_GROUPS
SSM_STATE = 128
CONV_WIDTH = 4
CONV_CH = SSM_WIDTH + 2 * SSM_GROUPS * SSM_STATE
SSD_CHUNK = 256
DIFF_HEADS = 4
DIFF_DIM = 64
DIFF_WIDTH = DIFF_HEADS * 2 * DIFF_DIM
TOP_EXPERTS = 4
SWIGLU_LIMIT = 7.0
SWIGLU_ALPHA = 1.702
ROPE_THETA = 500000.0
LN_EPS = 1e-5
RMS_EPS = 1e-6

LANES = 128
VMEM_LIMIT = 56 * 1024 * 1024
NEG_BIG = -1e30
LOG2E = 1.4426950408889634
KEY_FLOOR = -(2 ** 31) + 0x7FFFFF + 1
HALF16 = 2 ** 15

DSA_IN = 11 * LANES
DSA_OUT = 12 * LANES
SSM_COLS = SSM_WIDTH + CONV_CH + SSM_HEADS
SSM_PAD = 21 * LANES
DIFF_COLS = 3 * DIFF_WIDTH

bf16 = jnp.bfloat16
f32 = jnp.float32


def _cparams(sem, vmem=VMEM_LIMIT):
    return pltpu.CompilerParams(dimension_semantics=sem, vmem_limit_bytes=vmem)


def _split3(a):
    a1 = a.astype(bf16)
    r = a - a1.astype(f32)
    a2 = r.astype(bf16)
    a3 = (r - a2.astype(f32)).astype(bf16)
    return a1, a2, a3


def _dot(a, b):
    return jnp.dot(a, b, preferred_element_type=f32)


def _dot_nt(a, b):
    return lax.dot_general(a, b, (((1,), (1,)), ((), ())), preferred_element_type=f32)


def _dot_tn(a, b):
    return lax.dot_general(a, b, (((0,), (0,)), ((), ())), preferred_element_type=f32)


def _dot_f32(a, b3):
    a1, a2, a3 = _split3(a)
    b1, b2, b3_ = b3
    return (_dot(a1, b1) + (_dot(a1, b2) + _dot(a2, b1))
            + (_dot(a2, b2) + _dot(a1, b3_) + _dot(a3, b1)))


def _ada_kernel(c_ref, w_ref, b_ref, o_ref):
    c = c_ref[...]
    c = c * (1.0 / (1.0 + jnp.exp(-c)))
    w1, w2, w3 = _split3(w_ref[0])
    o_ref[0] = _dot_f32(c, (w1, w2, w3)) + b_ref[0]


def _ada_mod(c, ada_w, ada_b):
    depth, d, n = ada_w.shape
    bsz = c.shape[0]
    rows = 8
    cp = jnp.zeros((rows, d), f32).at[:bsz].set(c)
    tn = 512
    out = pl.pallas_call(
        _ada_kernel,
        grid=(depth, n // tn),
        in_specs=[pl.BlockSpec((rows, d), lambda l, j: (0, 0)),
                  pl.BlockSpec((1, d, tn), lambda l, j: (l, 0, j)),
                  pl.BlockSpec((1, 1, tn), lambda l, j: (l, 0, j))],
        out_specs=pl.BlockSpec((1, rows, tn), lambda l, j: (l, 0, j)),
        out_shape=jax.ShapeDtypeStruct((depth, rows, n), f32),
        compiler_params=_cparams(("parallel", "parallel")),
        name="ada_mod",
    )(cp, ada_w, ada_b.reshape(depth, 1, n))
    return out[:, :bsz]


def _rope_tables(positions, head_dim):
    rot = head_dim // 4
    half = rot // 2
    pos = positions.reshape(-1).astype(f32)
    inv_freq = ROPE_THETA ** (-jnp.arange(half, dtype=f32) / half)
    ang = pos[:, None] * inv_freq[None, :]
    cos, sin = jnp.cos(ang), jnp.sin(ang)
    lane = jnp.arange(LANES)
    q = lane % head_dim
    j = q % half
    in_rot = q < rot
    first = q < half
    c = jnp.where(in_rot[None, :], cos[:, j], 1.0)
    s1 = jnp.where(first[None, :], -sin[:, j], 0.0)
    s2 = jnp.where((in_rot & ~first)[None, :], sin[:, j], 0.0)
    return c.astype(f32), s1.astype(f32), s2.astype(f32)


def _rot_block(blk, c, s1, s2, half):
    return (blk * c + pltpu.roll(blk, LANES - half, 1) * s1
            + pltpu.roll(blk, half, 1) * s2)


def _modulate(x_ref, sc_ref, sh_ref):
    return (x_ref[...] * (1.0 + sc_ref[0]) + sh_ref[0]).astype(bf16)


def _proj_dsa_kernel(x_ref, sc_ref, sh_ref, w_ref, c128, s1_128, s2_128, c64, s1_64, s2_64,
                     lng_ref, lnb_ref, o_ref, iw_ref):
    h = _modulate(x_ref, sc_ref, sh_ref)
    acc = _dot(h, w_ref[...])
    t128 = (c128[...], s1_128[...], s2_128[...])
    t64 = (c64[...], s1_64[...], s2_64[...])
    a_scale = HEAD_DIM ** -0.5 * LOG2E
    for cb in range(4):
        blk = acc[:, cb * LANES:(cb + 1) * LANES]
        o_ref[:, cb * LANES:(cb + 1) * LANES] = (_rot_block(blk, *t128, 16) * a_scale).astype(bf16)
    for cb in range(4, 8):
        blk = acc[:, cb * LANES:(cb + 1) * LANES]
        o_ref[:, cb * LANES:(cb + 1) * LANES] = _rot_block(blk, *t64, 8).astype(bf16)
    blk = acc[:, 8 * LANES:9 * LANES]
    o_ref[:, 8 * LANES:9 * LANES] = _rot_block(blk, *t128, 16).astype(bf16)
    o_ref[:, 9 * LANES:10 * LANES] = acc[:, 9 * LANES:10 * LANES].astype(bf16)
    blk = acc[:, 10 * LANES:11 * LANES]
    lane = lax.broadcasted_iota(jnp.int32, blk.shape, 1)
    is_k = lane < IDX_DIM
    mu = jnp.sum(jnp.where(is_k, blk, 0.0), axis=1, keepdims=True) * (1.0 / IDX_DIM)
    d = jnp.where(is_k, blk - mu, 0.0)
    var = jnp.sum(d * d, axis=1, keepdims=True) * (1.0 / IDX_DIM)
    kn = d * lax.rsqrt(var + LN_EPS) * lng_ref[...] + lnb_ref[...]
    kn = jnp.where(is_k, _rot_block(kn, *t64, 8), 0.0)
    o_ref[:, 10 * LANES:11 * LANES] = kn.astype(bf16)
    o_ref[:, 11 * LANES:12 * LANES] = pltpu.roll(kn, IDX_DIM, 1).astype(bf16)
    iw_ref[...] = pltpu.roll(blk, IDX_DIM, 1) * (IDX_HEADS ** -0.5 * IDX_DIM ** -0.5)


def _proj_ssm_kernel(x_ref, sc_ref, sh_ref, w_ref, o_ref):
    h = _modulate(x_ref, sc_ref, sh_ref)
    o_ref[...] = _dot(h, w_ref[...])


def _proj_diff_kernel(x_ref, sc_ref, sh_ref, w_ref, c64, s1_64, s2_64, o_ref):
    h = _modulate(x_ref, sc_ref, sh_ref)
    acc = _dot(h, w_ref[...])
    t64 = (c64[...], s1_64[...], s2_64[...])
    q_scale = DIFF_DIM ** -0.5 * LOG2E
    for cb in range(4):
        blk = acc[:, cb * LANES:(cb + 1) * LANES]
        o_ref[:, cb * LANES:(cb + 1) * LANES] = (_rot_block(blk, *t64, 8) * q_scale).astype(bf16)
    for cb in range(4, 8):
        blk = acc[:, cb * LANES:(cb + 1) * LANES]
        o_ref[:, cb * LANES:(cb + 1) * LANES] = _rot_block(blk, *t64, 8).astype(bf16)
    o_ref[:, 8 * LANES:] = acc[:, 8 * LANES:].astype(bf16)


def _row_specs(tm, d, tiles_per_batch):
    return [pl.BlockSpec((tm, d), lambda i: (i, 0)),
            pl.BlockSpec((1, 1, d), lambda i: (i // tiles_per_batch, 0, 0)),
            pl.BlockSpec((1, 1, d), lambda i: (i // tiles_per_batch, 0, 0))]


def _tab_spec(tm):
    return pl.BlockSpec((tm, LANES), lambda i: (i, 0))


def _full_spec(shape):
    nd = len(shape)
    return pl.BlockSpec(shape, lambda i: (0,) * nd)


def _proj_dsa(x2, scale, shift, w, t128, t64, lng, lnb, seq):
    t, d = x2.shape
    tm = 256
    return pl.pallas_call(
        _proj_dsa_kernel,
        grid=(t // tm,),
        in_specs=_row_specs(tm, d, seq // tm) + [_full_spec(w.shape)] + [_tab_spec(tm)] * 6
        + [_full_spec((1, LANES))] * 2,
        out_specs=[pl.BlockSpec((tm, DSA_OUT), lambda i: (i, 0)),
                   pl.BlockSpec((tm, LANES), lambda i: (i, 0))],
        out_shape=[jax.ShapeDtypeStruct((t, DSA_OUT), bf16),
                   jax.ShapeDtypeStruct((t, LANES), f32)],
        compiler_params=_cparams(("parallel",)),
        name="proj_dsa",
    )(x2, scale, shift, w, *t128, *t64, lng, lnb)


def _proj_ssm(x2, scale, shift, w, seq):
    t, d = x2.shape
    tm = 256
    return pl.pallas_call(
        _proj_ssm_kernel,
        grid=(t // tm,),
        in_specs=_row_specs(tm, d, seq // tm) + [_full_spec(w.shape)],
        out_specs=pl.BlockSpec((tm, SSM_PAD), lambda i: (i, 0)),
        out_shape=jax.ShapeDtypeStruct((t, SSM_PAD), f32),
        compiler_params=_cparams(("parallel",)),
        name="proj_ssm",
    )(x2, scale, shift, w)


def _proj_diff(x2, scale, shift, w, t64, seq):
    t, d = x2.shape
    tm = 256
    return pl.pallas_call(
        _proj_diff_kernel,
        grid=(t // tm,),
        in_specs=_row_specs(tm, d, seq // tm) + [_full_spec(w.shape)] + [_tab_spec(tm)] * 3,
        out_specs=pl.BlockSpec((tm, DIFF_COLS), lambda i: (i, 0)),
        out_shape=jax.ShapeDtypeStruct((t, DIFF_COLS), bf16),
        compiler_params=_cparams(("parallel",)),
        name="proj_diff",
    )(x2, scale, shift, w, *t64)


def _split_w_in(w_in_l, lng, lnb):
    d = w_in_l.shape[0]
    sizes = (A_WIDTH, HEAD_DIM, HEAD_DIM, IDX_HEADS * IDX_DIM, IDX_DIM, IDX_HEADS,
             SSM_WIDTH, CONV_CH, SSM_HEADS, DIFF_WIDTH, DIFF_WIDTH, DIFF_WIDTH)
    offs = [0]
    for s in sizes:
        offs.append(offs[-1] + s)
    col = lambda k: w_in_l[:, offs[k]:offs[k + 1]]
    pad = lambda n: jnp.zeros((d, n), w_in_l.dtype)
    w_dsa = jnp.concatenate([col(0), col(3), col(1), col(2), col(4), col(5),
                             pad(LANES - IDX_DIM - IDX_HEADS)], axis=1).astype(bf16)
    w_ssm = jnp.concatenate([col(6), col(7), col(8), pad(SSM_PAD - SSM_COLS)], axis=1).astype(bf16)
    w_diff = jnp.concatenate([col(9), col(10), col(11)], axis=1).astype(bf16)
    lng_p = jnp.zeros((1, LANES), f32).at[0, :IDX_DIM].set(lng)
    lnb_p = jnp.zeros((1, LANES), f32).at[0, :IDX_DIM].set(lnb)
    return w_dsa, w_ssm, w_diff, lng_p, lnb_p


def _sortable(x):
    b = lax.bitcast_convert_type(x, jnp.int32)
    return b ^ ((b >> 31) & jnp.int32(0x7FFFFFFF))


def _online_softmax_step(s, v_blk, carry):
    m, l, acc = carry
    m_new = jnp.maximum(m, jnp.max(s, axis=1, keepdims=True))
    alpha = jnp.exp2(m - m_new)
    p = jnp.exp2(s - m_new)
    l = alpha * l + jnp.sum(p, axis=1, keepdims=True)
    acc = alpha * acc + _dot(p.astype(bf16), v_blk)
    return m_new, l, acc


def _flash_pipelined(logits_fn, v_ref, n_chunks, tk, rows):
    def process(s, idx, carry):
        off = pl.multiple_of(idx * tk, tk)
        return _online_softmax_step(s, v_ref[pl.ds(off, tk), :], carry)

    init = (jnp.full((rows, 1), NEG_BIG, f32), jnp.zeros((rows, 1), f32),
            jnp.zeros((rows, LANES), f32))
    last = n_chunks - 1
    carry = process(logits_fn(last, True), last, init)

    def pair(t, carry):
        s_a = logits_fn(2 * t, False)
        s_b = logits_fn(2 * t + 1, False)
        return process(s_b, 2 * t + 1, process(s_a, 2 * t, carry))

    def single(kc, carry):
        return process(logits_fn(kc, False), kc, carry)

    n_pairs = last // 2
    carry = lax.fori_loop(0, n_pairs, pair, carry)
    _, l, acc = lax.fori_loop(2 * n_pairs, last, single, carry)
    return acc / l


def _dsa_kernel(aq_ref, iq_ref, ak_ref, av_ref, ika_ref, ikb_ref, iw_ref, o_ref,
                keys_ref, khi_ref, klo_ref, *, tq, tk, topk):
    i = pl.program_id(1)
    q_start = i * tq
    n_chunks = (q_start + tq + tk - 1) // tk
    row = q_start + lax.broadcasted_iota(jnp.int32, (tq, tk), 0)
    iw = iw_ref[...]
    w_cols = [iw[:, h:h + 1] for h in range(IDX_HEADS)]

    def score_chunk(kc, carry):
        off = pl.multiple_of(kc * tk, tk)
        ka = ika_ref[pl.ds(off, tk), :]
        kb = ikb_ref[pl.ds(off, tk), :]
        sc = jnp.zeros((tq, tk), f32)
        for m in range(IDX_HEADS // 2):
            qp = iq_ref[:, m * LANES:(m + 1) * LANES]
            sc = sc + w_cols[2 * m] * jnp.maximum(_dot_nt(qp, ka), 0.0)
            sc = sc + w_cols[2 * m + 1] * jnp.maximum(_dot_nt(qp, kb), 0.0)
        col = off + lax.broadcasted_iota(jnp.int32, (tq, tk), 1)
        key = _sortable(jnp.where(col <= row, sc, -jnp.inf))
        keys_ref[:, pl.ds(off, tk)] = key
        khi_ref[:, pl.ds(off, tk)] = (key >> 16).astype(jnp.int16)
        klo_ref[:, pl.ds(off, tk)] = ((key & 0xFFFF) - HALF16).astype(jnp.int16)
        return carry

    lax.fori_loop(0, n_chunks, score_chunk, 0)

    one16, zero16 = jnp.ones((), jnp.int16), jnp.zeros((), jnp.int16)

    def count_ge(ref, cand):
        c16 = cand.astype(jnp.int16)

        def body(kc, part):
            off = pl.multiple_of(kc * tk, tk)
            hit = jnp.where(ref[:, pl.ds(off, tk)] >= c16, one16, zero16)
            for c in range(tk // LANES):
                part = part + hit[:, c * LANES:(c + 1) * LANES]
            return part
        part = lax.fori_loop(0, n_chunks, body, jnp.zeros((tq, LANES), jnp.int16))
        return jnp.sum(part.astype(f32), axis=1, keepdims=True)

    kf = float(topk)
    floor16 = jnp.full((tq, 1), -HALF16, jnp.int32)

    def greedy(ref, base):
        def bit_body(b, t):
            cand = t + jnp.left_shift(jnp.int32(1), 15 - b)
            return jnp.where(base + count_ge(ref, cand) >= kf, cand, t)
        return lax.fori_loop(0, 16, bit_body, floor16)

    t_hi = greedy(khi_ref, 0.0)
    n_above = count_ge(khi_ref, t_hi + 1)
    t_hi16 = t_hi.astype(jnp.int16)

    def mask_chunk(kc, carry):
        off = pl.multiple_of(kc * tk, tk)
        sl = pl.ds(off, tk)
        klo_ref[:, sl] = jnp.where(khi_ref[:, sl] == t_hi16, klo_ref[:, sl], jnp.int16(-HALF16))
        return carry

    lax.fori_loop(0, n_chunks, mask_chunk, 0)
    t_lo = greedy(klo_ref, n_above)
    thr = jnp.maximum(t_hi * 65536 + (t_lo + HALF16), KEY_FLOOR)

    q4 = jnp.concatenate([aq_ref[:, h * LANES:(h + 1) * LANES] for h in range(A_HEADS)], axis=0)

    def logits(kc, diagonal):
        del diagonal
        off = pl.multiple_of(kc * tk, tk)
        bias = jnp.where(keys_ref[:, pl.ds(off, tk)] >= thr, 0.0, NEG_BIG)
        return _dot_nt(q4, ak_ref[pl.ds(off, tk), :]) + jnp.concatenate([bias] * A_HEADS, axis=0)

    o = _flash_pipelined(logits, av_ref, n_chunks, tk, A_HEADS * tq)
    for h in range(A_HEADS):
        o_ref[:, h * LANES:(h + 1) * LANES] = o[h * tq:(h + 1) * tq].astype(bf16)


def _dsa_attention(qkv, iw, bsz, seq):
    tq = min(128, seq)
    tk = min(512, seq)
    nq = seq // tq
    topk = min(TOPK_MAX, seq // 4)
    qspec = lambda c: pl.BlockSpec((tq, A_WIDTH), lambda b, i: (b * nq + i, c))
    kspec = lambda c: pl.BlockSpec((seq, LANES), lambda b, i: (b, c), pipeline_mode=pl.Buffered(1))
    return pl.pallas_call(
        functools.partial(_dsa_kernel, tq=tq, tk=tk, topk=topk),
        grid=(bsz, nq),
        in_specs=[qspec(0), qspec(1), kspec(8), kspec(9), kspec(10), kspec(11),
                  pl.BlockSpec((tq, LANES), lambda b, i: (b * nq + i, 0))],
        out_specs=pl.BlockSpec((tq, A_WIDTH), lambda b, i: (b * nq + i, 0)),
        out_shape=jax.ShapeDtypeStruct((bsz * seq, A_WIDTH), bf16),
        scratch_shapes=[pltpu.VMEM((tq, seq), jnp.int32), pltpu.VMEM((tq, seq), jnp.int16),
                        pltpu.VMEM((tq, seq), jnp.int16)],
        compiler_params=_cparams(("parallel", "arbitrary")),
        name="dsa_attention",
    )(qkv, qkv, qkv, qkv, qkv, qkv, iw)


def _diff_kernel(q_ref, k_ref, v_ref, lam_ref, g_ref, o_ref, *, tq, tk, lam_init):
    i = pl.program_id(2)
    q_start = i * tq
    n_chunks = (q_start + tq + tk - 1) // tk
    qf = q_ref[...].astype(f32)
    lane = lax.broadcasted_iota(jnp.int32, qf.shape, 1)
    q2 = jnp.concatenate([jnp.where(lane < DIFF_DIM, qf, 0.0),
                          jnp.where(lane >= DIFF_DIM, qf, 0.0)], axis=0).astype(bf16)

    def logits(kc, diagonal):
        off = pl.multiple_of(kc * tk, tk)
        s = _dot_nt(q2, k_ref[pl.ds(off, tk), :])
        if diagonal:
            row = q_start + lax.broadcasted_iota(jnp.int32, (tq, tk), 0)
            col = off + lax.broadcasted_iota(jnp.int32, (2 * tq, tk), 1)
            s = jnp.where(col <= jnp.concatenate([row, row], axis=0), s, NEG_BIG)
        return s

    o = _flash_pipelined(logits, v_ref, n_chunks, tk, 2 * tq)
    lp = lam_ref[...]
    lam = (jnp.exp(jnp.sum(lp[0:1] * lp[1:2], axis=1, keepdims=True))
           - jnp.exp(jnp.sum(lp[2:3] * lp[3:4], axis=1, keepdims=True)) + lam_init)
    o = o[:tq] - lam * o[tq:]
    ms = jnp.mean(o * o, axis=1, keepdims=True)
    o_ref[...] = (o * lax.rsqrt(ms + RMS_EPS) * g_ref[...] * (1.0 - lam_init)).astype(bf16)


def _diff_attention(qkv, diff_lambda_l, diff_norm_g_l, lam_init, bsz, seq):
    tq = min(256, seq)
    tk = min(512, seq)
    nq = seq // tq
    return pl.pallas_call(
        functools.partial(_diff_kernel, tq=tq, tk=tk, lam_init=lam_init),
        grid=(bsz, DIFF_HEADS, nq),
        in_specs=[pl.BlockSpec((tq, LANES), lambda b, h, i: (b * nq + i, h)),
                  pl.BlockSpec((seq, LANES), lambda b, h, i: (b, DIFF_HEADS + h)),
                  pl.BlockSpec((seq, LANES), lambda b, h, i: (b, 2 * DIFF_HEADS + h)),
                  pl.BlockSpec((4, DIFF_DIM), lambda b, h, i: (0, 0)),
                  pl.BlockSpec((1, LANES), lambda b, h, i: (0, 0))],
        out_specs=pl.BlockSpec((tq, LANES), lambda b, h, i: (b * nq + i, h)),
        out_shape=jax.ShapeDtypeStruct((bsz * seq, DIFF_WIDTH), bf16),
        compiler_params=_cparams(("parallel", "parallel", "arbitrary")),
        name="diff_attention",
    )(qkv, qkv, qkv, diff_lambda_l, diff_norm_g_l.reshape(1, LANES))


def _sigmoid(x):
    return 1.0 / (1.0 + jnp.exp(-x))


def _expand_heads(v, e_ref):
    v1, v2, v3 = _split3(v)
    e = e_ref[...]
    return _dot(v1, e) + _dot(v2, e) + _dot(v3, e)


def _ssd_kernel(s_ref, cw_ref, cb_ref, dtb_ref, alog_ref, dskip_ref, ng_ref, e_ref, o_ref,
                xpad_ref, state_ref, *, q):
    hp = SSM_HEADS_PER_GROUP * SSM_HEAD_DIM
    xo, do = SSM_WIDTH, SSM_WIDTH + CONV_CH

    @pl.when(pl.program_id(1) == 0)
    def _():
        xpad_ref[0:8, :] = jnp.zeros((8, CONV_CH), f32)
        state_ref[...] = jnp.zeros(state_ref.shape, f32)

    xpad_ref[8:, :] = s_ref[:, xo:do]
    cw = cw_ref[...]
    y = cb_ref[...] + cw[3:4] * xpad_ref[8:q + 8, :]
    y = y + cw[2:3] * xpad_ref[7:q + 7, :]
    y = y + cw[1:2] * xpad_ref[6:q + 6, :]
    y = y + cw[0:1] * xpad_ref[5:q + 5, :]
    xpad_ref[0:8, :] = xpad_ref[q:q + 8, :]
    xa = y * _sigmoid(y)
    xs = xa[:, :SSM_WIDTH]

    dtr = s_ref[:, do:do + LANES] + dtb_ref[...]
    dt = jnp.maximum(dtr, 0.0) + jnp.log1p(jnp.exp(-jnp.abs(dtr)))
    a = dt * (-jnp.exp(alog_ref[...]))
    ri = lax.broadcasted_iota(jnp.int32, (q, q), 0)
    ci = lax.broadcasted_iota(jnp.int32, (q, q), 1)
    tril = ri >= ci
    tri = jnp.where(tril, 1.0, 0.0).astype(bf16)
    a1, a2, a3 = _split3(a)
    a_cum = _dot(tri, a1) + _dot(tri, a2) + _dot(tri, a3)
    a_cum_t = a_cum.T
    a_last = a_cum[q - 1:q, :]
    dt_x = _expand_heads(dt, e_ref)
    ea_x = _expand_heads(jnp.exp(a_cum), e_ref)
    dte_x = _expand_heads(jnp.exp(a_last - a_cum), e_ref)
    xdt = xs * dt_x
    xw = (xdt * dte_x).astype(bf16)
    lane = lax.broadcasted_iota(jnp.int32, (q, LANES), 1)
    lo_half = lane < SSM_HEAD_DIM

    outs = []
    for g in range(SSM_GROUPS):
        bg = xa[:, SSM_WIDTH + g * SSM_STATE:SSM_WIDTH + (g + 1) * SSM_STATE].astype(bf16)
        cg = xa[:, SSM_WIDTH + (SSM_GROUPS + g) * SSM_STATE:
                SSM_WIDTH + (SSM_GROUPS + g + 1) * SSM_STATE].astype(bf16)
        cbm = _dot_nt(cg, bg)
        pieces = []
        for pr in range(SSM_HEADS_PER_GROUP // 2):
            c0 = g * hp + pr * LANES
            xpair = xdt[:, c0:c0 + LANES]
            acc = None
            for sub in range(2):
                h = g * SSM_HEADS_PER_GROUP + 2 * pr + sub
                seg = a_cum[:, h:h + 1] - a_cum_t[h:h + 1, :]
                dec = jnp.where(tril, jnp.exp(seg), 0.0)
                mm = (cbm * dec).astype(bf16)
                keep = lo_half if sub == 0 else jnp.logical_not(lo_half)
                part = _dot(mm, jnp.where(keep, xpair, 0.0).astype(bf16))
                acc = part if acc is None else acc + part
            pieces.append(acc)
        y_diag = jnp.concatenate(pieces, axis=1)
        prev = state_ref[g]
        y_off = _dot(cg, prev.astype(bf16)) * ea_x[:, g * hp:(g + 1) * hp]
        new_states = _dot_tn(bg, xw[:, g * hp:(g + 1) * hp])
        state_ref[g] = prev * ea_x[q - 1:q, g * hp:(g + 1) * hp] + new_states
        outs.append(y_diag + y_off)
    yv = jnp.concatenate(outs, axis=1) + dskip_ref[...] * xs
    z = s_ref[:, :SSM_WIDTH]
    yv = yv * (z * _sigmoid(z))
    ng = ng_ref[...]
    for g in range(SSM_GROUPS):
        yg = yv[:, g * hp:(g + 1) * hp]
        ms = jnp.mean(yg * yg, axis=1, keepdims=True)
        o_ref[:, g * hp:(g + 1) * hp] = (yg * lax.rsqrt(ms + RMS_EPS)
                                         * ng[:, g * hp:(g + 1) * hp]).astype(bf16)


def _ssd_mixer(ssm, conv_w, conv_b, dt_bias, a_log, d_skip, norm_g, bsz, seq):
    q = math.gcd(SSD_CHUNK, seq)
    nc = seq // q
    pad16 = lambda v: jnp.zeros((1, LANES), f32).at[0, :SSM_HEADS].set(v)
    head_of_lane = jnp.arange(SSM_WIDTH) // SSM_HEAD_DIM
    expand = (jnp.arange(LANES)[:, None] == head_of_lane[None, :]).astype(bf16)
    const = lambda shape: pl.BlockSpec(shape, lambda b, c: (0,) * len(shape))
    return pl.pallas_call(
        functools.partial(_ssd_kernel, q=q),
        grid=(bsz, nc),
        in_specs=[pl.BlockSpec((q, SSM_PAD), lambda b, c: (b * nc + c, 0)),
                  const((CONV_WIDTH, CONV_CH)), const((1, CONV_CH)), const((1, LANES)),
                  const((1, LANES)), const((1, SSM_WIDTH)), const((1, SSM_WIDTH)),
                  const((LANES, SSM_WIDTH))],
        out_specs=pl.BlockSpec((q, SSM_WIDTH), lambda b, c: (b * nc + c, 0)),
        out_shape=jax.ShapeDtypeStruct((bsz * seq, SSM_WIDTH), bf16),
        scratch_shapes=[pltpu.VMEM((q + 8, CONV_CH), f32),
                        pltpu.VMEM((SSM_GROUPS, SSM_STATE, SSM_HEADS_PER_GROUP * SSM_HEAD_DIM), f32)],
        compiler_params=_cparams(("parallel", "arbitrary")),
        name="ssd_mixer",
    )(ssm, conv_w, conv_b.reshape(1, CONV_CH), pad16(dt_bias), pad16(a_log),
      jnp.repeat(d_skip, SSM_HEAD_DIM).reshape(1, SSM_WIDTH), norm_g.reshape(1, SSM_WIDTH), expand)


def _layer_norm_rows(r, g, b):
    mu = jnp.mean(r, axis=1, keepdims=True)
    d = r - mu
    var = jnp.mean(d * d, axis=1, keepdims=True)
    return d * lax.rsqrt(var + LN_EPS) * g + b


def _outproj_kernel(oa_ref, ob_ref, oc_ref, w_ref, x_ref, gate_ref, g_ref, b_ref, o_ref, *, alpha):
    y = _dot(oa_ref[...], w_ref[0:A_WIDTH, :])
    y = y + _dot(ob_ref[...], w_ref[A_WIDTH:A_WIDTH + SSM_WIDTH, :])
    y = y + _dot(oc_ref[...], w_ref[A_WIDTH + SSM_WIDTH:, :])
    r = alpha * x_ref[...] + gate_ref[0] * y
    o_ref[...] = _layer_norm_rows(r, g_ref[...], b_ref[...])


def _out_proj(oa, ob, oc, w_out_b, x2, gate, ln_g, ln_b, alpha, seq):
    t, d = x2.shape
    tm = 256
    tpb = seq // tm
    rows = lambda w: pl.BlockSpec((tm, w), lambda i: (i, 0))
    return pl.pallas_call(
        functools.partial(_outproj_kernel, alpha=alpha),
        grid=(t // tm,),
        in_specs=[rows(A_WIDTH), rows(SSM_WIDTH), rows(DIFF_WIDTH), _full_spec(w_out_b.shape),
                  rows(d), pl.BlockSpec((1, 1, d), lambda i: (i // tpb, 0, 0)),
                  _full_spec((1, d)), _full_spec((1, d))],
        out_specs=rows(d),
        out_shape=jax.ShapeDtypeStruct((t, d), f32),
        compiler_params=_cparams(("parallel",)),
        name="out_proj_ln",
    )(oa, ob, oc, w_out_b, x2, gate, ln_g.reshape(1, d), ln_b.reshape(1, d))


def _router_kernel(x_ref, sc_ref, sh_ref, w1_ref, w2_ref, w3_ref, b_ref, h_ref, idx_ref, gate_ref,
                   *, n_experts):
    h = x_ref[...] * (1.0 + sc_ref[0]) + sh_ref[0]
    h_ref[...] = h.astype(bf16)
    logits = _dot_f32(h, (w1_ref[...], w2_ref[...], w3_ref[...])) + b_ref[...]
    lane = lax.broadcasted_iota(jnp.int32, logits.shape, 1)
    lane_f = lane.astype(f32)
    cur = jnp.where(lane < n_experts, logits, -jnp.inf)
    vals, idxs = [], []
    for _ in range(TOP_EXPERTS):
        m = jnp.max(cur, axis=1, keepdims=True)
        ix = jnp.min(jnp.where(cur == m, lane_f, float(LANES)), axis=1, keepdims=True)
        vals.append(m)
        idxs.append(ix)
        cur = jnp.where(lane_f == ix, -jnp.inf, cur)
    es = [jnp.exp(v - vals[0]) for v in vals]
    tot = es[0] + es[1] + es[2] + es[3]
    gates = jnp.zeros(logits.shape, f32)
    idx_o = jnp.zeros(logits.shape, f32)
    for k in range(TOP_EXPERTS):
        gates = jnp.where(lane == k, es[k] / tot, gates)
        idx_o = jnp.where(lane == k, idxs[k], idx_o)
    gate_ref[...] = gates
    idx_ref[...] = idx_o.astype(jnp.int32)


def _router(x2, scale, shift, router_w_l, router_b_l, seq):
    t, d = x2.shape
    n_experts = router_w_l.shape[1]
    tm = 256
    wp = jnp.zeros((d, LANES), f32).at[:, :n_experts].set(router_w_l)
    w1, w2, w3 = _split3(wp)
    bp = jnp.zeros((1, LANES), f32).at[0, :n_experts].set(router_b_l)
    rows = lambda w: pl.BlockSpec((tm, w), lambda i: (i, 0))
    return pl.pallas_call(
        functools.partial(_router_kernel, n_experts=n_experts),
        grid=(t // tm,),
        in_specs=_row_specs(tm, d, seq // tm) + [_full_spec((d, LANES))] * 3 + [_full_spec((1, LANES))],
        out_specs=[rows(d), rows(LANES), rows(LANES)],
        out_shape=[jax.ShapeDtypeStruct((t, d), bf16),
                   jax.ShapeDtypeStruct((t, LANES), jnp.int32),
                   jax.ShapeDtypeStruct((t, LANES), f32)],
        compiler_params=_cparams(("parallel",)),
        name="router",
    )(x2, scale, shift, w1, w2, w3, bp)


MOE_TM = 1024
MOE_TN = 512


def _expert_up_kernel(be_ref, nb_ref, x_ref, wg_ref, wu_ref, bg_ref, bu_ref, o_ref):
    live = pl.program_id(0) < nb_ref[0]

    @pl.when(live)
    def _():
        x = x_ref[...]
        g = _dot(x, wg_ref[0].astype(bf16)) + bg_ref[0]
        u = _dot(x, wu_ref[0].astype(bf16)) + bu_ref[0]
        g = jnp.minimum(g, SWIGLU_LIMIT)
        u = jnp.clip(u, -SWIGLU_LIMIT, SWIGLU_LIMIT)
        o_ref[...] = (g * _sigmoid(SWIGLU_ALPHA * g) * (u + 1.0)).astype(bf16)

    @pl.when(jnp.logical_not(live))
    def _():
        o_ref[...] = jnp.zeros(o_ref.shape, bf16)


def _expert_down_kernel(be_ref, nb_ref, a_ref, wd_ref, bd_ref, o_ref):
    live = pl.program_id(0) < nb_ref[0]

    @pl.when(live)
    def _():
        o_ref[...] = (_dot(a_ref[...], wd_ref[0].astype(bf16)) + bd_ref[0]).astype(bf16)

    @pl.when(jnp.logical_not(live))
    def _():
        o_ref[...] = jnp.zeros(o_ref.shape, bf16)


def _expert_ffn(xs, block_expert, n_live, w_gu, b_gu, w_down, b_down, layer):
    n_slots, d = xs.shape
    depth, n_exp, _, two_f = w_gu.shape
    fdim = two_f // 2
    tm = MOE_TM
    th, tn = min(MOE_TN, fdim), min(MOE_TN, d)
    nj, nd = fdim // th, d // tn
    nb = n_slots // tm
    rowmap = lambda b, j, be, nl: (jnp.minimum(b, nl[0] - 1), 0)
    jsel = lambda b, j, nl, n: jnp.where(b < nl[0], j, n - 1)
    up_spec = pltpu.PrefetchScalarGridSpec(
        num_scalar_prefetch=2,
        grid=(nb, nj),
        in_specs=[pl.BlockSpec((tm, d), rowmap),
                  pl.BlockSpec((None, 1, d, th), lambda b, j, be, nl: (layer, be[b], 0, jsel(b, j, nl, nj))),
                  pl.BlockSpec((None, 1, d, th), lambda b, j, be, nl: (layer, be[b], 0, nj + jsel(b, j, nl, nj))),
                  pl.BlockSpec((None, 1, 1, th), lambda b, j, be, nl: (layer, be[b], 0, jsel(b, j, nl, nj))),
                  pl.BlockSpec((None, 1, 1, th), lambda b, j, be, nl: (layer, be[b], 0, nj + jsel(b, j, nl, nj)))],
        out_specs=pl.BlockSpec((tm, th), lambda b, j, be, nl: (b, j)),
    )
    b_gu4 = b_gu.reshape(depth, n_exp, 1, two_f)
    act = pl.pallas_call(
        _expert_up_kernel,
        grid_spec=up_spec,
        out_shape=jax.ShapeDtypeStruct((n_slots, fdim), bf16),
        compiler_params=_cparams(("arbitrary", "arbitrary")),
        name="expert_up",
    )(block_expert, n_live, xs, w_gu, w_gu, b_gu4, b_gu4)
    down_spec = pltpu.PrefetchScalarGridSpec(
        num_scalar_prefetch=2,
        grid=(nb, nd),
        in_specs=[pl.BlockSpec((tm, fdim), rowmap),
                  pl.BlockSpec((None, 1, fdim, tn), lambda b, j, be, nl: (layer, be[b], 0, jsel(b, j, nl, nd))),
                  pl.BlockSpec((None, 1, 1, tn), lambda b, j, be, nl: (layer, be[b], 0, jsel(b, j, nl, nd)))],
        out_specs=pl.BlockSpec((tm, tn), lambda b, j, be, nl: (b, j)),
    )
    return pl.pallas_call(
        _expert_down_kernel,
        grid_spec=down_spec,
        out_shape=jax.ShapeDtypeStruct((n_slots, d), bf16),
        compiler_params=_cparams(("arbitrary", "arbitrary")),
        name="expert_down",
    )(block_expert, n_live, act, w_down, b_down.reshape(depth, n_exp, 1, d))


def _moe_dispatch(top_idx, n_tok, n_experts):
    tm = MOE_TM
    n_assign = n_tok * TOP_EXPERTS
    flat_e = top_idx.reshape(-1)
    onehot = (flat_e[:, None] == jnp.arange(n_experts, dtype=jnp.int32)[None, :]).astype(jnp.int32)
    before = jnp.cumsum(onehot, axis=0) - onehot
    rank = jnp.sum(before * onehot, axis=1)
    counts = jnp.sum(onehot, axis=0)
    padded = ((counts + tm - 1) // tm) * tm
    pend = jnp.cumsum(padded)
    pstart = pend - padded
    dest = pstart[flat_e] + rank
    n_blocks = -(-n_assign // tm) + n_experts
    n_slots = n_blocks * tm
    flat_tok = jnp.arange(n_assign, dtype=jnp.int32) // TOP_EXPERTS
    slot_tok = jnp.full((n_slots,), n_tok, jnp.int32).at[dest].set(flat_tok)
    n_live = (pend[-1] // tm).astype(jnp.int32)
    blk_start = jnp.minimum(jnp.arange(n_blocks, dtype=jnp.int32), n_live - 1) * tm
    block_expert = jnp.searchsorted(pend, blk_start, side='right').astype(jnp.int32)
    block_expert = jnp.minimum(block_expert, n_experts - 1)
    return dest, slot_tok, block_expert, n_live.reshape(1)


def _final_kernel(x_ref, y_ref, rg_ref, gate_ref, g_ref, b_ref, o_ref, *, alpha):
    rg = rg_ref[...]
    y = rg[:, 0:1] * y_ref[0].astype(f32)
    for k in range(1, TOP_EXPERTS):
        y = y + rg[:, k:k + 1] * y_ref[k].astype(f32)
    r = alpha * x_ref[...] + gate_ref[0] * y
    o_ref[...] = _layer_norm_rows(r, g_ref[...], b_ref[...])


def _final_ln(x2, y4, rgates, gate, ln_g, ln_b, alpha, seq):
    t, d = x2.shape
    tm = 256
    tpb = seq // tm
    return pl.pallas_call(
        functools.partial(_final_kernel, alpha=alpha),
        grid=(t // tm,),
        in_specs=[pl.BlockSpec((tm, d), lambda i: (i, 0)),
                  pl.BlockSpec((TOP_EXPERTS, tm, d), lambda i: (0, i, 0)),
                  pl.BlockSpec((tm, LANES), lambda i: (i, 0)),
                  pl.BlockSpec((1, 1, d), lambda i: (i // tpb, 0, 0)),
                  _full_spec((1, d)), _full_spec((1, d))],
        out_specs=pl.BlockSpec((tm, d), lambda i: (i, 0)),
        out_shape=jax.ShapeDtypeStruct((t, d), f32),
        compiler_params=_cparams(("parallel",)),
        name="moe_sum_ln",
    )(x2, y4, rgates, gate, ln_g.reshape(1, d), ln_b.reshape(1, d))


def kernel(x, c, positions, w_in, w_out, idx_ln_g, idx_ln_b, conv_w, conv_b, dt_bias, a_log,
           d_skip, ssm_norm_g, diff_lambda, diff_norm_g, ada_w, ada_b, ln_g, ln_b,
           router_w, router_b, w_gu, b_gu, w_down, b_down):
    bsz, seq, d = x.shape
    depth = w_in.shape[0]
    n_tok = bsz * seq
    n_experts = router_w.shape[-1]
    alpha = (2 * depth) ** 0.25

    mod = _ada_mod(c, ada_w, ada_b).reshape(depth, bsz, 6, 1, d)
    t128 = _rope_tables(positions, HEAD_DIM)
    t64 = _rope_tables(positions, IDX_DIM)
    x2 = x.reshape(n_tok, d)

    for layer in range(depth):
        shift_m, scale_m, gate_m, shift_f, scale_f, gate_f = (mod[layer, :, k] for k in range(6))
        w_dsa, w_ssm, w_diff, lng_p, lnb_p = _split_w_in(w_in[layer], idx_ln_g[layer], idx_ln_b[layer])

        qkv_a, iw = _proj_dsa(x2, scale_m, shift_m, w_dsa, t128, t64, lng_p, lnb_p, seq)
        out_a = _dsa_attention(qkv_a, iw, bsz, seq)
        ssm = _proj_ssm(x2, scale_m, shift_m, w_ssm, seq)
        out_b = _ssd_mixer(ssm, conv_w[layer], conv_b[layer], dt_bias[layer], a_log[layer],
                           d_skip[layer], ssm_norm_g[layer], bsz, seq)
        qkv_c = _proj_diff(x2, scale_m, shift_m, w_diff, t64, seq)
        lam_init = 0.8 - 0.6 * math.exp(-0.3 * layer)
        out_c = _diff_attention(qkv_c, diff_lambda[layer], diff_norm_g[layer], lam_init, bsz, seq)
        x2 = _out_proj(out_a, out_b, out_c, w_out[layer].astype(bf16), x2, gate_m,
                       ln_g[layer, 0], ln_b[layer, 0], alpha, seq)

        hf, top_idx, gates = _router(x2, scale_f, shift_f, router_w[layer], router_b[layer], seq)
        dest, slot_tok, block_expert, n_live = _moe_dispatch(top_idx[:, :TOP_EXPERTS], n_tok, n_experts)
        xs = hf[jnp.minimum(slot_tok, n_tok - 1)]
        ys = _expert_ffn(xs, block_expert, n_live, w_gu, b_gu, w_down, b_down, layer)
        y4 = ys[dest.reshape(n_tok, TOP_EXPERTS).T]
        x2 = _final_ln(x2, y4, gates, gate_f, ln_g[layer, 1], ln_b[layer, 1], alpha, seq)

    return x2.reshape(bsz, seq, d)
```

```python
import functools
import math

import jax
import jax.numpy as jnp
from jax import lax
from jax.experimental import pallas as pl
from jax.experimental.pallas import tpu as pltpu

HEAD_DIM = 128
A_HEADS = 4
A_WIDTH = A_HEADS * HEAD_DIM
IDX_HEADS = 8
IDX_DIM = 64
TOPK_MAX = 256
SSM_HEADS = 16
SSM_HEAD_DIM = 64
SSM_WIDTH = SSM_HEADS * SSM_HEAD_DIM
SSM_GROUPS = 2
SSM_HEADS_PER_GROUP = SSM_HEADS // SSM_GROUPS
SSM_STATE = 128
CONV_WIDTH = 4
CONV_CH = SSM_WIDTH + 2 * SSM_GROUPS * SSM_STATE
SSD_CHUNK = 256
DIFF_HEADS = 4
DIFF_DIM = 64
DIFF_WIDTH = DIFF_HEADS * 2 * DIFF_DIM
TOP_EXPERTS = 4
SWIGLU_LIMIT = 7.0
SWIGLU_ALPHA = 1.702
ROPE_THETA = 500000.0
LN_EPS = 1e-5
RMS_EPS = 1e-6

LANES = 128
VMEM_LIMIT = 56 * 1024 * 1024
NEG_BIG = -1e30
LOG2E = 1.4426950408889634
KEY_FLOOR = -(2 ** 31) + 0x7FFFFF + 1
HALF16 = 2 ** 15

DSA_IN = 11 * LANES
DSA_OUT = 12 * LANES
SSM_COLS = SSM_WIDTH + CONV_CH + SSM_HEADS
SSM_PAD = 21 * LANES
DIFF_COLS = 3 * DIFF_WIDTH

bf16 = jnp.bfloat16
f32 = jnp.float32


def _cparams(sem, vmem=VMEM_LIMIT):
    return pltpu.CompilerParams(dimension_semantics=sem, vmem_limit_bytes=vmem)


def _split3(a):
    a1 = a.astype(bf16)
    r = a - a1.astype(f32)
    a2 = r.astype(bf16)
    a3 = (r - a2.astype(f32)).astype(bf16)
    return a1, a2, a3


def _dot(a, b):
    return jnp.dot(a, b, preferred_element_type=f32)


def _dot_nt(a, b):
    return lax.dot_general(a, b, (((1,), (1,)), ((), ())), preferred_element_type=f32)


def _dot_tn(a, b):
    return lax.dot_general(a, b, (((0,), (0,)), ((), ())), preferred_element_type=f32)


def _dot_f32(a, b3):
    a1, a2, a3 = _split3(a)
    b1, b2, b3_ = b3
    return (_dot(a1, b1) + (_dot(a1, b2) + _dot(a2, b1))
            + (_dot(a2, b2) + _dot(a1, b3_) + _dot(a3, b1)))


def _ada_kernel(c_ref, w_ref, b_ref, o_ref):
    c = c_ref[...]
    c = c * (1.0 / (1.0 + jnp.exp(-c)))
    w1, w2, w3 = _split3(w_ref[0])
    o_ref[0] = _dot_f32(c, (w1, w2, w3)) + b_ref[0]


def _ada_mod(c, ada_w, ada_b):
    depth, d, n = ada_w.shape
    bsz = c.shape[0]
    rows = 8
    cp = jnp.zeros((rows, d), f32).at[:bsz].set(c)
    tn = 512
    out = pl.pallas_call(
        _ada_kernel,
        grid=(depth, n // tn),
        in_specs=[pl.BlockSpec((rows, d), lambda l, j: (0, 0)),
                  pl.BlockSpec((1, d, tn), lambda l, j: (l, 0, j)),
                  pl.BlockSpec((1, 1, tn), lambda l, j: (l, 0, j))],
        out_specs=pl.BlockSpec((1, rows, tn), lambda l, j: (l, 0, j)),
        out_shape=jax.ShapeDtypeStruct((depth, rows, n), f32),
        compiler_params=_cparams(("parallel", "parallel")),
        name="ada_mod",
    )(cp, ada_w, ada_b.reshape(depth, 1, n))
    return out[:, :bsz]


def _rope_tables(positions, head_dim):
    rot = head_dim // 4
    half = rot // 2
    pos = positions.reshape(-1).astype(f32)
    inv_freq = ROPE_THETA ** (-jnp.arange(half, dtype=f32) / half)
    ang = pos[:, None] * inv_freq[None, :]
    cos, sin = jnp.cos(ang), jnp.sin(ang)
    lane = jnp.arange(LANES)
    q = lane % head_dim
    j = q % half
    in_rot = q < rot
    first = q < half
    c = jnp.where(in_rot[None, :], cos[:, j], 1.0)
    s1 = jnp.where(first[None, :], -sin[:, j], 0.0)
    s2 = jnp.where((in_rot & ~first)[None, :], sin[:, j], 0.0)
    return c.astype(f32), s1.astype(f32), s2.astype(f32)


def _rot_block(blk, c, s1, s2, half):
    return (blk * c + pltpu.roll(blk, LANES - half, 1) * s1
            + pltpu.roll(blk, half, 1) * s2)


def _modulate(x_ref, sc_ref, sh_ref):
    return (x_ref[...] * (1.0 + sc_ref[0]) + sh_ref[0]).astype(bf16)


def _proj_dsa_kernel(x_ref, sc_ref, sh_ref, w_ref, c128, s1_128, s2_128, c64, s1_64, s2_64,
                     lng_ref, lnb_ref, o_ref, iw_ref):
    h = _modulate(x_ref, sc_ref, sh_ref)
    acc = _dot(h, w_ref[...])
    t128 = (c128[...], s1_128[...], s2_128[...])
    t64 = (c64[...], s1_64[...], s2_64[...])
    a_scale = HEAD_DIM ** -0.5 * LOG2E
    for cb in range(4):
        blk = acc[:, cb * LANES:(cb + 1) * LANES]
        o_ref[:, cb * LANES:(cb + 1) * LANES] = (_rot_block(blk, *t128, 16) * a_scale).astype(bf16)
    for cb in range(4, 8):
        blk = acc[:, cb * LANES:(cb + 1) * LANES]
        o_ref[:, cb * LANES:(cb + 1) * LANES] = _rot_block(blk, *t64, 8).astype(bf16)
    blk = acc[:, 8 * LANES:9 * LANES]
    o_ref[:, 8 * LANES:9 * LANES] = _rot_block(blk, *t128, 16).astype(bf16)
    o_ref[:, 9 * LANES:10 * LANES] = acc[:, 9 * LANES:10 * LANES].astype(bf16)
    blk = acc[:, 10 * LANES:11 * LANES]
    lane = lax.broadcasted_iota(jnp.int32, blk.shape, 1)
    is_k = lane < IDX_DIM
    mu = jnp.sum(jnp.where(is_k, blk, 0.0), axis=1, keepdims=True) * (1.0 / IDX_DIM)
    d = jnp.where(is_k, blk - mu, 0.0)
    var = jnp.sum(d * d, axis=1, keepdims=True) * (1.0 / IDX_DIM)
    kn = d * lax.rsqrt(var + LN_EPS) * lng_ref[...] + lnb_ref[...]
    kn = jnp.where(is_k, _rot_block(kn, *t64, 8), 0.0)
    o_ref[:, 10 * LANES:11 * LANES] = kn.astype(bf16)
    o_ref[:, 11 * LANES:12 * LANES] = pltpu.roll(kn, IDX_DIM, 1).astype(bf16)
    iw_ref[...] = pltpu.roll(blk, IDX_DIM, 1) * (IDX_HEADS ** -0.5 * IDX_DIM ** -0.5)


def _proj_ssm_kernel(x_ref, sc_ref, sh_ref, w_ref, o_ref):
    h = _modulate(x_ref, sc_ref, sh_ref)
    o_ref[...] = _dot(h, w_ref[...])


def _proj_diff_kernel(x_ref, sc_ref, sh_ref, w_ref, c64, s1_64, s2_64, o_ref):
    h = _modulate(x_ref, sc_ref, sh_ref)
    acc = _dot(h, w_ref[...])
    t64 = (c64[...], s1_64[...], s2_64[...])
    q_scale = DIFF_DIM ** -0.5 * LOG2E
    for cb in range(4):
        blk = acc[:, cb * LANES:(cb + 1) * LANES]
        o_ref[:, cb * LANES:(cb + 1) * LANES] = (_rot_block(blk, *t64, 8) * q_scale).astype(bf16)
    for cb in range(4, 8):
        blk = acc[:, cb * LANES:(cb + 1) * LANES]
        o_ref[:, cb * LANES:(cb + 1) * LANES] = _rot_block(blk, *t64, 8).astype(bf16)
    o_ref[:, 8 * LANES:] = acc[:, 8 * LANES:].astype(bf16)


def _row_specs(tm, d, tiles_per_batch):
    return [pl.BlockSpec((tm, d), lambda i: (i, 0)),
            pl.BlockSpec((1, 1, d), lambda i: (i // tiles_per_batch, 0, 0)),
            pl.BlockSpec((1, 1, d), lambda i: (i // tiles_per_batch, 0, 0))]


def _tab_spec(tm):
    return pl.BlockSpec((tm, LANES), lambda i: (i, 0))


def _full_spec(shape):
    nd = len(shape)
    return pl.BlockSpec(shape, lambda i: (0,) * nd)


def _proj_dsa(x2, scale, shift, w, t128, t64, lng, lnb, seq):
    t, d = x2.shape
    tm = 256
    return pl.pallas_call(
        _proj_dsa_kernel,
        grid=(t // tm,),
        in_specs=_row_specs(tm, d, seq // tm) + [_full_spec(w.shape)] + [_tab_spec(tm)] * 6
        + [_full_spec((1, LANES))] * 2,
        out_specs=[pl.BlockSpec((tm, DSA_OUT), lambda i: (i, 0)),
                   pl.BlockSpec((tm, LANES), lambda i: (i, 0))],
        out_shape=[jax.ShapeDtypeStruct((t, DSA_OUT), bf16),
                   jax.ShapeDtypeStruct((t, LANES), f32)],
        compiler_params=_cparams(("parallel",)),
        name="proj_dsa",
    )(x2, scale, shift, w, *t128, *t64, lng, lnb)


def _proj_ssm(x2, scale, shift, w, seq):
    t, d = x2.shape
    tm = 256
    return pl.pallas_call(
        _proj_ssm_kernel,
        grid=(t // tm,),
        in_specs=_row_specs(tm, d, seq // tm) + [_full_spec(w.shape)],
        out_specs=pl.BlockSpec((tm, SSM_PAD), lambda i: (i, 0)),
        out_shape=jax.ShapeDtypeStruct((t, SSM_PAD), f32),
        compiler_params=_cparams(("parallel",)),
        name="proj_ssm",
    )(x2, scale, shift, w)


def _proj_diff(x2, scale, shift, w, t64, seq):
    t, d = x2.shape
    tm = 256
    return pl.pallas_call(
        _proj_diff_kernel,
        grid=(t // tm,),
        in_specs=_row_specs(tm, d, seq // tm) + [_full_spec(w.shape)] + [_tab_spec(tm)] * 3,
        out_specs=pl.BlockSpec((tm, DIFF_COLS), lambda i: (i, 0)),
        out_shape=jax.ShapeDtypeStruct((t, DIFF_COLS), bf16),
        compiler_params=_cparams(("parallel",)),
        name="proj_diff",
    )(x2, scale, shift, w, *t64)


def _split_w_in(w_in_l, lng, lnb):
    d = w_in_l.shape[0]
    sizes = (A_WIDTH, HEAD_DIM, HEAD_DIM, IDX_HEADS * IDX_DIM, IDX_DIM, IDX_HEADS,
             SSM_WIDTH, CONV_CH, SSM_HEADS, DIFF_WIDTH, DIFF_WIDTH, DIFF_WIDTH)
    offs = [0]
    for s in sizes:
        offs.append(offs[-1] + s)
    col = lambda k: w_in_l[:, offs[k]:offs[k + 1]]
    pad = lambda n: jnp.zeros((d, n), w_in_l.dtype)
    w_dsa = jnp.concatenate([col(0), col(3), col(1), col(2), col(4), col(5),
                             pad(LANES - IDX_DIM - IDX_HEADS)], axis=1).astype(bf16)
    w_ssm = jnp.concatenate([col(6), col(7), col(8), pad(SSM_PAD - SSM_COLS)], axis=1).astype(bf16)
    w_diff = jnp.concatenate([col(9), col(10), col(11)], axis=1).astype(bf16)
    lng_p = jnp.zeros((1, LANES), f32).at[0, :IDX_DIM].set(lng)
    lnb_p = jnp.zeros((1, LANES), f32).at[0, :IDX_DIM].set(lnb)
    return w_dsa, w_ssm, w_diff, lng_p, lnb_p


def _sortable(x):
    b = lax.bitcast_convert_type(x, jnp.int32)
    return b ^ ((b >> 31) & jnp.int32(0x7FFFFFFF))


def _online_softmax_step(s, v_blk, carry):
    m, l, acc = carry
    m_new = jnp.maximum(m, jnp.max(s, axis=1, keepdims=True))
    alpha = jnp.exp2(m - m_new)
    p = jnp.exp2(s - m_new)
    l = alpha * l + jnp.sum(p, axis=1, keepdims=True)
    acc = alpha * acc + _dot(p.astype(bf16), v_blk)
    return m_new, l, acc


def _flash_pipelined(logits_fn, v_ref, n_chunks, tk, rows):
    def process(s, idx, carry):
        off = pl.multiple_of(idx * tk, tk)
        return _online_softmax_step(s, v_ref[pl.ds(off, tk), :], carry)

    init = (jnp.full((rows, 1), NEG_BIG, f32), jnp.zeros((rows, 1), f32),
            jnp.zeros((rows, LANES), f32))
    last = n_chunks - 1
    carry = process(logits_fn(last, True), last, init)

    def pair(t, carry):
        s_a = logits_fn(2 * t, False)
        s_b = logits_fn(2 * t + 1, False)
        return process(s_b, 2 * t + 1, process(s_a, 2 * t, carry))

    def single(kc, carry):
        return process(logits_fn(kc, False), kc, carry)

    n_pairs = last // 2
    carry = lax.fori_loop(0, n_pairs, pair, carry)
    _, l, acc = lax.fori_loop(2 * n_pairs, last, single, carry)
    return acc / l


def _dsa_kernel(aq_ref, iq_ref, ak_ref, av_ref, ika_ref, ikb_ref, iw_ref, o_ref,
                keys_ref, khi_ref, klo_ref, *, tq, tk, topk):
    i = pl.program_id(1)
    q_start = i * tq
    n_chunks = (q_start + tq + tk - 1) // tk
    row = q_start + lax.broadcasted_iota(jnp.int32, (tq, tk), 0)
    iw = iw_ref[...]
    w_cols = [iw[:, h:h + 1] for h in range(IDX_HEADS)]
    iq4 = jnp.concatenate([iq_ref[:, m * LANES:(m + 1) * LANES] for m in range(IDX_HEADS // 2)],
                          axis=0)

    def score_chunk(kc, carry):
        off = pl.multiple_of(kc * tk, tk)
        ka = ika_ref[pl.ds(off, tk), :]
        kb = ikb_ref[pl.ds(off, tk), :]
        ra = _dot_nt(iq4, ka)
        rb = _dot_nt(iq4, kb)
        sc = jnp.zeros((tq, tk), f32)
        for m in range(IDX_HEADS // 2):
            sc = sc + w_cols[2 * m] * jnp.maximum(ra[m * tq:(m + 1) * tq], 0.0)
            sc = sc + w_cols[2 * m + 1] * jnp.maximum(rb[m * tq:(m + 1) * tq], 0.0)
        col = off + lax.broadcasted_iota(jnp.int32, (tq, tk), 1)
        key = _sortable(jnp.where(col <= row, sc, -jnp.inf))
        keys_ref[:, pl.ds(off, tk)] = key
        khi_ref[:, pl.ds(off, tk)] = (key >> 16).astype(jnp.int16)
        klo_ref[:, pl.ds(off, tk)] = ((key & 0xFFFF) - HALF16).astype(jnp.int16)
        return carry

    lax.fori_loop(0, n_chunks, score_chunk, 0)

    one16, zero16 = jnp.ones((), jnp.int16), jnp.zeros((), jnp.int16)

    def count_ge(ref, cand):
        c16 = cand.astype(jnp.int16)

        def body(kc, part):
            off = pl.multiple_of(kc * tk, tk)
            hit = jnp.where(ref[:, pl.ds(off, tk)] >= c16, one16, zero16)
            for c in range(tk // LANES):
                part = part + hit[:, c * LANES:(c + 1) * LANES]
            return part
        part = lax.fori_loop(0, n_chunks, body, jnp.zeros((tq, LANES), jnp.int16))
        return jnp.sum(part.astype(f32), axis=1, keepdims=True)

    kf = float(topk)
    floor16 = jnp.full((tq, 1), -HALF16, jnp.int32)

    def greedy(ref, base):
        def bit_body(b, t):
            cand = t + jnp.left_shift(jnp.int32(1), 15 - b)
            return jnp.where(base + count_ge(ref, cand) >= kf, cand, t)
        return lax.fori_loop(0, 16, bit_body, floor16)

    t_hi = greedy(khi_ref, 0.0)
    n_above = count_ge(khi_ref, t_hi + 1)
    t_hi16 = t_hi.astype(jnp.int16)

    def mask_chunk(kc, carry):
        off = pl.multiple_of(kc * tk, tk)
        sl = pl.ds(off, tk)
        klo_ref[:, sl] = jnp.where(khi_ref[:, sl] == t_hi16, klo_ref[:, sl], jnp.int16(-HALF16))
        return carry

    lax.fori_loop(0, n_chunks, mask_chunk, 0)
    t_lo = greedy(klo_ref, n_above)
    thr = jnp.maximum(t_hi * 65536 + (t_lo + HALF16), KEY_FLOOR)

    q4 = jnp.concatenate([aq_ref[:, h * LANES:(h + 1) * LANES] for h in range(A_HEADS)], axis=0)

    def logits(kc, diagonal):
        del diagonal
        off = pl.multiple_of(kc * tk, tk)
        bias = jnp.where(keys_ref[:, pl.ds(off, tk)] >= thr, 0.0, NEG_BIG)
        return _dot_nt(q4, ak_ref[pl.ds(off, tk), :]) + jnp.concatenate([bias] * A_HEADS, axis=0)

    o = _flash_pipelined(logits, av_ref, n_chunks, tk, A_HEADS * tq)
    for h in range(A_HEADS):
        o_ref[:, h * LANES:(h + 1) * LANES] = o[h * tq:(h + 1) * tq].astype(bf16)


def _dsa_attention(qkv, iw, bsz, seq):
    tq = min(128, seq)
    tk = min(512, seq)
    nq = seq // tq
    topk = min(TOPK_MAX, seq // 4)
    qspec = lambda c: pl.BlockSpec((tq, A_WIDTH), lambda b, i: (b * nq + i, c))
    kspec = lambda c: pl.BlockSpec((seq, LANES), lambda b, i: (b, c), pipeline_mode=pl.Buffered(1))
    return pl.pallas_call(
        functools.partial(_dsa_kernel, tq=tq, tk=tk, topk=topk),
        grid=(bsz, nq),
        in_specs=[qspec(0), qspec(1), kspec(8), kspec(9), kspec(10), kspec(11),
                  pl.BlockSpec((tq, LANES), lambda b, i: (b * nq + i, 0))],
        out_specs=pl.BlockSpec((tq, A_WIDTH), lambda b, i: (b * nq + i, 0)),
        out_shape=jax.ShapeDtypeStruct((bsz * seq, A_WIDTH), bf16),
        scratch_shapes=[pltpu.VMEM((tq, seq), jnp.int32), pltpu.VMEM((tq, seq), jnp.int16),
                        pltpu.VMEM((tq, seq), jnp.int16)],
        compiler_params=_cparams(("parallel", "arbitrary")),
        name="dsa_attention",
    )(qkv, qkv, qkv, qkv, qkv, qkv, iw)


def _diff_kernel(q_ref, k_ref, v_ref, lam_ref, g_ref, o_ref, *, tq, tk, lam_init):
    i = pl.program_id(2)
    q_start = i * tq
    n_chunks = (q_start + tq + tk - 1) // tk
    qf = q_ref[...].astype(f32)
    lane = lax.broadcasted_iota(jnp.int32, qf.shape, 1)
    q2 = jnp.concatenate([jnp.where(lane < DIFF_DIM, qf, 0.0),
                          jnp.where(lane >= DIFF_DIM, qf, 0.0)], axis=0).astype(bf16)

    def logits(kc, diagonal):
        off = pl.multiple_of(kc * tk, tk)
        s = _dot_nt(q2, k_ref[pl.ds(off, tk), :])
        if diagonal:
            row = q_start + lax.broadcasted_iota(jnp.int32, (tq, tk), 0)
            col = off + lax.broadcasted_iota(jnp.int32, (2 * tq, tk), 1)
            s = jnp.where(col <= jnp.concatenate([row, row], axis=0), s, NEG_BIG)
        return s

    o = _flash_pipelined(logits, v_ref, n_chunks, tk, 2 * tq)
    lp = lam_ref[...]
    lam = (jnp.exp(jnp.sum(lp[0:1] * lp[1:2], axis=1, keepdims=True))
           - jnp.exp(jnp.sum(lp[2:3] * lp[3:4], axis=1, keepdims=True)) + lam_init)
    o = o[:tq] - lam * o[tq:]
    ms = jnp.mean(o * o, axis=1, keepdims=True)
    o_ref[...] = (o * lax.rsqrt(ms + RMS_EPS) * g_ref[...] * (1.0 - lam_init)).astype(bf16)


def _diff_attention(qkv, diff_lambda_l, diff_norm_g_l, lam_init, bsz, seq):
    tq = min(256, seq)
    tk = min(512, seq)
    nq = seq // tq
    return pl.pallas_call(
        functools.partial(_diff_kernel, tq=tq, tk=tk, lam_init=lam_init),
        grid=(bsz, DIFF_HEADS, nq),
        in_specs=[pl.BlockSpec((tq, LANES), lambda b, h, i: (b * nq + i, h)),
                  pl.BlockSpec((seq, LANES), lambda b, h, i: (b, DIFF_HEADS + h)),
                  pl.BlockSpec((seq, LANES), lambda b, h, i: (b, 2 * DIFF_HEADS + h)),
                  pl.BlockSpec((4, DIFF_DIM), lambda b, h, i: (0, 0)),
                  pl.BlockSpec((1, LANES), lambda b, h, i: (0, 0))],
        out_specs=pl.BlockSpec((tq, LANES), lambda b, h, i: (b * nq + i, h)),
        out_shape=jax.ShapeDtypeStruct((bsz * seq, DIFF_WIDTH), bf16),
        compiler_params=_cparams(("parallel", "parallel", "arbitrary")),
        name="diff_attention",
    )(qkv, qkv, qkv, diff_lambda_l, diff_norm_g_l.reshape(1, LANES))


def _sigmoid(x):
    return 1.0 / (1.0 + jnp.exp(-x))


def _expand_heads(v, e_ref):
    v1, v2, v3 = _split3(v)
    e = e_ref[...]
    return _dot(v1, e) + _dot(v2, e) + _dot(v3, e)


def _ssd_kernel(s_ref, cw_ref, cb_ref, dtb_ref, alog_ref, dskip_ref, ng_ref, e_ref, o_ref,
                xpad_ref, state_ref, *, q):
    hp = SSM_HEADS_PER_GROUP * SSM_HEAD_DIM
    xo, do = SSM_WIDTH, SSM_WIDTH + CONV_CH

    @pl.when(pl.program_id(1) == 0)
    def _():
        xpad_ref[0:8, :] = jnp.zeros((8, CONV_CH), f32)
        state_ref[...] = jnp.zeros(state_ref.shape, f32)

    xpad_ref[8:, :] = s_ref[:, xo:do]
    cw = cw_ref[...]
    y = cb_ref[...] + cw[3:4] * xpad_ref[8:q + 8, :]
    y = y + cw[2:3] * xpad_ref[7:q + 7, :]
    y = y + cw[1:2] * xpad_ref[6:q + 6, :]
    y = y + cw[0:1] * xpad_ref[5:q + 5, :]
    xpad_ref[0:8, :] = xpad_ref[q:q + 8, :]
    xa = y * _sigmoid(y)
    xs = xa[:, :SSM_WIDTH]

    dtr = s_ref[:, do:do + LANES] + dtb_ref[...]
    dt = jnp.maximum(dtr, 0.0) + jnp.log1p(jnp.exp(-jnp.abs(dtr)))
    a = dt * (-jnp.exp(alog_ref[...]))
    ri = lax.broadcasted_iota(jnp.int32, (q, q), 0)
    ci = lax.broadcasted_iota(jnp.int32, (q, q), 1)
    tril = ri >= ci
    tri = jnp.where(tril, 1.0, 0.0).astype(bf16)
    a1, a2, a3 = _split3(a)
    a_cum = _dot(tri, a1) + _dot(tri, a2) + _dot(tri, a3)
    a_cum_t = a_cum.T
    a_last = a_cum[q - 1:q, :]
    dt_x = _expand_heads(dt, e_ref)
    ea_x = _expand_heads(jnp.exp(a_cum), e_ref)
    dte_x = _expand_heads(jnp.exp(a_last - a_cum), e_ref)
    xdt = xs * dt_x
    xw = (xdt * dte_x).astype(bf16)
    lane = lax.broadcasted_iota(jnp.int32, (q, LANES), 1)
    lo_half = lane < SSM_HEAD_DIM

    outs = []
    for g in range(SSM_GROUPS):
        bg = xa[:, SSM_WIDTH + g * SSM_STATE:SSM_WIDTH + (g + 1) * SSM_STATE].astype(bf16)
        cg = xa[:, SSM_WIDTH + (SSM_GROUPS + g) * SSM_STATE:
                SSM_WIDTH + (SSM_GROUPS + g + 1) * SSM_STATE].astype(bf16)
        cbm = _dot_nt(cg, bg)
        pieces = []
        for pr in range(SSM_HEADS_PER_GROUP // 2):
            c0 = g * hp + pr * LANES
            xpair = xdt[:, c0:c0 + LANES]
            acc = None
            for sub in range(2):
                h = g * SSM_HEADS_PER_GROUP + 2 * pr + sub
                seg = a_cum[:, h:h + 1] - a_cum_t[h:h + 1, :]
                dec = jnp.where(tril, jnp.exp(seg), 0.0)
                mm = (cbm * dec).astype(bf16)
                keep = lo_half if sub == 0 else jnp.logical_not(lo_half)
                part = _dot(mm, jnp.where(keep, xpair, 0.0).astype(bf16))
                acc = part if acc is None else acc + part
            pieces.append(acc)
        y_diag = jnp.concatenate(pieces, axis=1)
        prev = state_ref[g]
        y_off = _dot(cg, prev.astype(bf16)) * ea_x[:, g * hp:(g + 1) * hp]
        new_states = _dot_tn(bg, xw[:, g * hp:(g + 1) * hp])
        state_ref[g] = prev * ea_x[q - 1:q, g * hp:(g + 1) * hp] + new_states
        outs.append(y_diag + y_off)
    yv = jnp.concatenate(outs, axis=1) + dskip_ref[...] * xs
    z = s_ref[:, :SSM_WIDTH]
    yv = yv * (z * _sigmoid(z))
    ng = ng_ref[...]
    for g in range(SSM_GROUPS):
        yg = yv[:, g * hp:(g + 1) * hp]
        ms = jnp.mean(yg * yg, axis=1, keepdims=True)
        o_ref[:, g * hp:(g + 1) * hp] = (yg * lax.rsqrt(ms + RMS_EPS)
                                         * ng[:, g * hp:(g + 1) * hp]).astype(bf16)


def _ssd_mixer(ssm, conv_w, conv_b, dt_bias, a_log, d_skip, norm_g, bsz, seq):
    q = math.gcd(SSD_CHUNK, seq)
    nc = seq // q
    pad16 = lambda v: jnp.zeros((1, LANES), f32).at[0, :SSM_HEADS].set(v)
    head_of_lane = jnp.arange(SSM_WIDTH) // SSM_HEAD_DIM
    expand = (jnp.arange(LANES)[:, None] == head_of_lane[None, :]).astype(bf16)
    const = lambda shape: pl.BlockSpec(shape, lambda b, c: (0,) * len(shape))
    return pl.pallas_call(
        functools.partial(_ssd_kernel, q=q),
        grid=(bsz, nc),
        in_specs=[pl.BlockSpec((q, SSM_PAD), lambda b, c: (b * nc + c, 0)),
                  const((CONV_WIDTH, CONV_CH)), const((1, CONV_CH)), const((1, LANES)),
                  const((1, LANES)), const((1, SSM_WIDTH)), const((1, SSM_WIDTH)),
                  const((LANES, SSM_WIDTH))],
        out_specs=pl.BlockSpec((q, SSM_WIDTH), lambda b, c: (b * nc + c, 0)),
        out_shape=jax.ShapeDtypeStruct((bsz * seq, SSM_WIDTH), bf16),
        scratch_shapes=[pltpu.VMEM((q + 8, CONV_CH), f32),
                        pltpu.VMEM((SSM_GROUPS, SSM_STATE, SSM_HEADS_PER_GROUP * SSM_HEAD_DIM), f32)],
        compiler_params=_cparams(("parallel", "arbitrary")),
        name="ssd_mixer",
    )(ssm, conv_w, conv_b.reshape(1, CONV_CH), pad16(dt_bias), pad16(a_log),
      jnp.repeat(d_skip, SSM_HEAD_DIM).reshape(1, SSM_WIDTH), norm_g.reshape(1, SSM_WIDTH), expand)


def _layer_norm_rows(r, g, b):
    mu = jnp.mean(r, axis=1, keepdims=True)
    d = r - mu
    var = jnp.mean(d * d, axis=1, keepdims=True)
    return d * lax.rsqrt(var + LN_EPS) * g + b


def _outproj_kernel(oa_ref, ob_ref, oc_ref, w_ref, x_ref, gate_ref, g_ref, b_ref, o_ref, *, alpha):
    y = _dot(oa_ref[...], w_ref[0:A_WIDTH, :])
    y = y + _dot(ob_ref[...], w_ref[A_WIDTH:A_WIDTH + SSM_WIDTH, :])
    y = y + _dot(oc_ref[...], w_ref[A_WIDTH + SSM_WIDTH:, :])
    r = alpha * x_ref[...] + gate_ref[0] * y
    o_ref[...] = _layer_norm_rows(r, g_ref[...], b_ref[...])


def _out_proj(oa, ob, oc, w_out_b, x2, gate, ln_g, ln_b, alpha, seq):
    t, d = x2.shape
    tm = 256
    tpb = seq // tm
    rows = lambda w: pl.BlockSpec((tm, w), lambda i: (i, 0))
    return pl.pallas_call(
        functools.partial(_outproj_kernel, alpha=alpha),
        grid=(t // tm,),
        in_specs=[rows(A_WIDTH), rows(SSM_WIDTH), rows(DIFF_WIDTH), _full_spec(w_out_b.shape),
                  rows(d), pl.BlockSpec((1, 1, d), lambda i: (i // tpb, 0, 0)),
                  _full_spec((1, d)), _full_spec((1, d))],
        out_specs=rows(d),
        out_shape=jax.ShapeDtypeStruct((t, d), f32),
        compiler_params=_cparams(("parallel",)),
        name="out_proj_ln",
    )(oa, ob, oc, w_out_b, x2, gate, ln_g.reshape(1, d), ln_b.reshape(1, d))


def _router_kernel(x_ref, sc_ref, sh_ref, w1_ref, w2_ref, w3_ref, b_ref, h_ref, idx_ref, gate_ref,
                   *, n_experts):
    h = x_ref[...] * (1.0 + sc_ref[0]) + sh_ref[0]
    h_ref[...] = h.astype(bf16)
    logits = _dot_f32(h, (w1_ref[...], w2_ref[...], w3_ref[...])) + b_ref[...]
    lane = lax.broadcasted_iota(jnp.int32, logits.shape, 1)
    lane_f = lane.astype(f32)
    cur = jnp.where(lane < n_experts, logits, -jnp.inf)
    vals, idxs = [], []
    for _ in range(TOP_EXPERTS):
        m = jnp.max(cur, axis=1, keepdims=True)
        ix = jnp.min(jnp.where(cur == m, lane_f, float(LANES)), axis=1, keepdims=True)
        vals.append(m)
        idxs.append(ix)
        cur = jnp.where(lane_f == ix, -jnp.inf, cur)
    es = [jnp.exp(v - vals[0]) for v in vals]
    tot = es[0] + es[1] + es[2] + es[3]
    gates = jnp.zeros(logits.shape, f32)
    idx_o = jnp.zeros(logits.shape, f32)
    for k in range(TOP_EXPERTS):
        gates = jnp.where(lane == k, es[k] / tot, gates)
        idx_o = jnp.where(lane == k, idxs[k], idx_o)
    gate_ref[...] = gates
    idx_ref[...] = idx_o.astype(jnp.int32)


def _router(x2, scale, shift, router_w_l, router_b_l, seq):
    t, d = x2.shape
    n_experts = router_w_l.shape[1]
    tm = 256
    wp = jnp.zeros((d, LANES), f32).at[:, :n_experts].set(router_w_l)
    w1, w2, w3 = _split3(wp)
    bp = jnp.zeros((1, LANES), f32).at[0, :n_experts].set(router_b_l)
    rows = lambda w: pl.BlockSpec((tm, w), lambda i: (i, 0))
    return pl.pallas_call(
        functools.partial(_router_kernel, n_experts=n_experts),
        grid=(t // tm,),
        in_specs=_row_specs(tm, d, seq // tm) + [_full_spec((d, LANES))] * 3 + [_full_spec((1, LANES))],
        out_specs=[rows(d), rows(LANES), rows(LANES)],
        out_shape=[jax.ShapeDtypeStruct((t, d), bf16),
                   jax.ShapeDtypeStruct((t, LANES), jnp.int32),
                   jax.ShapeDtypeStruct((t, LANES), f32)],
        compiler_params=_cparams(("parallel",)),
        name="router",
    )(x2, scale, shift, w1, w2, w3, bp)


MOE_TM = 1024
MOE_TN = 512


def _expert_up_kernel(be_ref, nb_ref, x_ref, wg_ref, wu_ref, bg_ref, bu_ref, o_ref):
    live = pl.program_id(0) < nb_ref[0]

    @pl.when(live)
    def _():
        x = x_ref[...]
        g = _dot(x, wg_ref[0].astype(bf16)) + bg_ref[0]
        u = _dot(x, wu_ref[0].astype(bf16)) + bu_ref[0]
        g = jnp.minimum(g, SWIGLU_LIMIT)
        u = jnp.clip(u, -SWIGLU_LIMIT, SWIGLU_LIMIT)
        o_ref[...] = (g * _sigmoid(SWIGLU_ALPHA * g) * (u + 1.0)).astype(bf16)

    @pl.when(jnp.logical_not(live))
    def _():
        o_ref[...] = jnp.zeros(o_ref.shape, bf16)


def _expert_down_kernel(be_ref, nb_ref, a_ref, wd_ref, bd_ref, o_ref):
    live = pl.program_id(0) < nb_ref[0]

    @pl.when(live)
    def _():
        o_ref[...] = (_dot(a_ref[...], wd_ref[0].astype(bf16)) + bd_ref[0]).astype(bf16)

    @pl.when(jnp.logical_not(live))
    def _():
        o_ref[...] = jnp.zeros(o_ref.shape, bf16)


def _expert_ffn(xs, block_expert, n_live, w_gu, b_gu, w_down, b_down, layer):
    n_slots, d = xs.shape
    depth, n_exp, _, two_f = w_gu.shape
    fdim = two_f // 2
    tm = MOE_TM
    th, tn = min(MOE_TN, fdim), min(MOE_TN, d)
    nj, nd = fdim // th, d // tn
    nb = n_slots // tm
    rowmap = lambda b, j, be, nl: (jnp.minimum(b, nl[0] - 1), 0)
    jsel = lambda b, j, nl, n: jnp.where(b < nl[0], j, n - 1)
    up_spec = pltpu.PrefetchScalarGridSpec(
        num_scalar_prefetch=2,
        grid=(nb, nj),
        in_specs=[pl.BlockSpec((tm, d), rowmap),
                  pl.BlockSpec((None, 1, d, th), lambda b, j, be, nl: (layer, be[b], 0, jsel(b, j, nl, nj))),
                  pl.BlockSpec((None, 1, d, th), lambda b, j, be, nl: (layer, be[b], 0, nj + jsel(b, j, nl, nj))),
                  pl.BlockSpec((None, 1, 1, th), lambda b, j, be, nl: (layer, be[b], 0, jsel(b, j, nl, nj))),
                  pl.BlockSpec((None, 1, 1, th), lambda b, j, be, nl: (layer, be[b], 0, nj + jsel(b, j, nl, nj)))],
        out_specs=pl.BlockSpec((tm, th), lambda b, j, be, nl: (b, j)),
    )
    b_gu4 = b_gu.reshape(depth, n_exp, 1, two_f)
    act = pl.pallas_call(
        _expert_up_kernel,
        grid_spec=up_spec,
        out_shape=jax.ShapeDtypeStruct((n_slots, fdim), bf16),
        compiler_params=_cparams(("arbitrary", "arbitrary")),
        name="expert_up",
    )(block_expert, n_live, xs, w_gu, w_gu, b_gu4, b_gu4)
    down_spec = pltpu.PrefetchScalarGridSpec(
        num_scalar_prefetch=2,
        grid=(nb, nd),
        in_specs=[pl.BlockSpec((tm, fdim), rowmap),
                  pl.BlockSpec((None, 1, fdim, tn), lambda b, j, be, nl: (layer, be[b], 0, jsel(b, j, nl, nd))),
                  pl.BlockSpec((None, 1, 1, tn), lambda b, j, be, nl: (layer, be[b], 0, jsel(b, j, nl, nd)))],
        out_specs=pl.BlockSpec((tm, tn), lambda b, j, be, nl: (b, j)),
    )
    return pl.pallas_call(
        _expert_down_kernel,
        grid_spec=down_spec,
        out_shape=jax.ShapeDtypeStruct((n_slots, d), bf16),
        compiler_params=_cparams(("arbitrary", "arbitrary")),
        name="expert_down",
    )(block_expert, n_live, act, w_down, b_down.reshape(depth, n_exp, 1, d))


def _moe_dispatch(top_idx, n_tok, n_experts):
    tm = MOE_TM
    n_assign = n_tok * TOP_EXPERTS
    flat_e = top_idx.reshape(-1)
    onehot = (flat_e[:, None] == jnp.arange(n_experts, dtype=jnp.int32)[None, :]).astype(jnp.int32)
    before = jnp.cumsum(onehot, axis=0) - onehot
    rank = jnp.sum(before * onehot, axis=1)
    counts = jnp.sum(onehot, axis=0)
    padded = ((counts + tm - 1) // tm) * tm
    pend = jnp.cumsum(padded)
    pstart = pend - padded
    dest = pstart[flat_e] + rank
    n_blocks = -(-n_assign // tm) + n_experts
    n_slots = n_blocks * tm
    flat_tok = jnp.arange(n_assign, dtype=jnp.int32) // TOP_EXPERTS
    slot_tok = jnp.full((n_slots,), n_tok, jnp.int32).at[dest].set(flat_tok)
    n_live = (pend[-1] // tm).astype(jnp.int32)
    blk_start = jnp.minimum(jnp.arange(n_blocks, dtype=jnp.int32), n_live - 1) * tm
    block_expert = jnp.sum((pend[None, :] <= blk_start[:, None]).astype(jnp.int32), axis=1)
    block_expert = jnp.minimum(block_expert, n_experts - 1)
    return dest, slot_tok, block_expert, n_live.reshape(1)


def _final_kernel(x_ref, y_ref, rg_ref, gate_ref, g_ref, b_ref, o_ref, *, alpha):
    rg = rg_ref[...]
    tm = rg.shape[0]
    shape = (tm, TOP_EXPERTS * tm)
    rank = (lax.broadcasted_iota(jnp.int32, shape, 1)
            - TOP_EXPERTS * lax.broadcasted_iota(jnp.int32, shape, 0))
    gmat = jnp.zeros(shape, f32)
    for k in range(TOP_EXPERTS):
        gmat = jnp.where(rank == k, rg[:, k:k + 1], gmat)
    g_hi = gmat.astype(bf16)
    g_lo = (gmat - g_hi.astype(f32)).astype(bf16)
    yb = y_ref[...]
    y = _dot(g_hi, yb) + _dot(g_lo, yb)
    r = alpha * x_ref[...] + gate_ref[0] * y
    o_ref[...] = _layer_norm_rows(r, g_ref[...], b_ref[...])


def _final_ln(x2, y4, rgates, gate, ln_g, ln_b, alpha, seq):
    t, d = x2.shape
    tm = 256
    tpb = seq // tm
    return pl.pallas_call(
        functools.partial(_final_kernel, alpha=alpha),
        grid=(t // tm,),
        in_specs=[pl.BlockSpec((tm, d), lambda i: (i, 0)),
                  pl.BlockSpec((TOP_EXPERTS * tm, d), lambda i: (i, 0)),
                  pl.BlockSpec((tm, LANES), lambda i: (i, 0)),
                  pl.BlockSpec((1, 1, d), lambda i: (i // tpb, 0, 0)),
                  _full_spec((1, d)), _full_spec((1, d))],
        out_specs=pl.BlockSpec((tm, d), lambda i: (i, 0)),
        out_shape=jax.ShapeDtypeStruct((t, d), f32),
        compiler_params=_cparams(("parallel",)),
        name="moe_sum_ln",
    )(x2, y4, rgates, gate, ln_g.reshape(1, d), ln_b.reshape(1, d))


def kernel(x, c, positions, w_in, w_out, idx_ln_g, idx_ln_b, conv_w, conv_b, dt_bias, a_log,
           d_skip, ssm_norm_g, diff_lambda, diff_norm_g, ada_w, ada_b, ln_g, ln_b,
           router_w, router_b, w_gu, b_gu, w_down, b_down):
    bsz, seq, d = x.shape
    depth = w_in.shape[0]
    n_tok = bsz * seq
    n_experts = router_w.shape[-1]
    alpha = (2 * depth) ** 0.25

    mod = _ada_mod(c, ada_w, ada_b).reshape(depth, bsz, 6, 1, d)
    t128 = _rope_tables(positions, HEAD_DIM)
    t64 = _rope_tables(positions, IDX_DIM)
    x2 = x.reshape(n_tok, d)

    for layer in range(depth):
        shift_m, scale_m, gate_m, shift_f, scale_f, gate_f = (mod[layer, :, k] for k in range(6))
        w_dsa, w_ssm, w_diff, lng_p, lnb_p = _split_w_in(w_in[layer], idx_ln_g[layer], idx_ln_b[layer])

        qkv_a, iw = _proj_dsa(x2, scale_m, shift_m, w_dsa, t128, t64, lng_p, lnb_p, seq)
        out_a = _dsa_attention(qkv_a, iw, bsz, seq)
        ssm = _proj_ssm(x2, scale_m, shift_m, w_ssm, seq)
        out_b = _ssd_mixer(ssm, conv_w[layer], conv_b[layer], dt_bias[layer], a_log[layer],
                           d_skip[layer], ssm_norm_g[layer], bsz, seq)
        qkv_c = _proj_diff(x2, scale_m, shift_m, w_diff, t64, seq)
        lam_init = 0.8 - 0.6 * math.exp(-0.3 * layer)
        out_c = _diff_attention(qkv_c, diff_lambda[layer], diff_norm_g[layer], lam_init, bsz, seq)
        x2 = _out_proj(out_a, out_b, out_c, w_out[layer].astype(bf16), x2, gate_m,
                       ln_g[layer, 0], ln_b[layer, 0], alpha, seq)

        hf, top_idx, gates = _router(x2, scale_f, shift_f, router_w[layer], router_b[layer], seq)
        dest, slot_tok, block_expert, n_live = _moe_dispatch(top_idx[:, :TOP_EXPERTS], n_tok, n_experts)
        xs = hf[jnp.minimum(slot_tok, n_tok - 1)]
        ys = _expert_ffn(xs, block_expert, n_live, w_gu, b_gu, w_down, b_down, layer)
        y4 = ys[dest]
        x2 = _final_ln(x2, y4, gates, gate_f, ln_g[layer, 1], ln_b[layer, 1], alpha, seq)

    return x2.reshape(bsz, seq, d)
```

```python
import functools
import math

import jax
import jax.numpy as jnp
from jax import lax
from jax.experimental import pallas as pl
from jax.experimental.pallas import tpu as pltpu

HEAD_DIM = 128
A_HEADS = 4
A_WIDTH = A_HEADS * HEAD_DIM
IDX_HEADS = 8
IDX_DIM = 64
TOPK_MAX = 256
SSM_HEADS = 16
SSM_HEAD_DIM = 64
SSM_WIDTH = SSM_HEADS * SSM_HEAD_DIM
SSM_GROUPS = 2
SSM_HEADS_PER_GROUP = SSM_HEADS // SSM_GROUPS
SSM_STATE = 128
CONV_WIDTH = 4
CONV_CH = SSM_WIDTH + 2 * SSM_GROUPS * SSM_STATE
SSD_CHUNK = 256
DIFF_HEADS = 4
DIFF_DIM = 64
DIFF_WIDTH = DIFF_HEADS * 2 * DIFF_DIM
TOP_EXPERTS = 4
SWIGLU_LIMIT = 7.0
SWIGLU_ALPHA = 1.702
ROPE_THETA = 500000.0
LN_EPS = 1e-5
RMS_EPS = 1e-6

LANES = 128
VMEM_LIMIT = 56 * 1024 * 1024
NEG_BIG = -1e30
LOG2E = 1.4426950408889634
KEY_FLOOR = -(2 ** 31) + 0x7FFFFF + 1
HALF16 = 2 ** 15

DSA_IN = 11 * LANES
DSA_OUT = 12 * LANES
SSM_COLS = SSM_WIDTH + CONV_CH + SSM_HEADS
SSM_PAD = 21 * LANES
DIFF_COLS = 3 * DIFF_WIDTH

bf16 = jnp.bfloat16
f32 = jnp.float32


def _cparams(sem, vmem=VMEM_LIMIT):
    return pltpu.CompilerParams(dimension_semantics=sem, vmem_limit_bytes=vmem)


def _split3(a):
    a1 = a.astype(bf16)
    r = a - a1.astype(f32)
    a2 = r.astype(bf16)
    a3 = (r - a2.astype(f32)).astype(bf16)
    return a1, a2, a3


def _dot(a, b):
    return jnp.dot(a, b, preferred_element_type=f32)


def _dot_nt(a, b):
    return lax.dot_general(a, b, (((1,), (1,)), ((), ())), preferred_element_type=f32)


def _dot_tn(a, b):
    return lax.dot_general(a, b, (((0,), (0,)), ((), ())), preferred_element_type=f32)


def _dot_f32(a, b3):
    a1, a2, a3 = _split3(a)
    b1, b2, b3_ = b3
    return (_dot(a1, b1) + (_dot(a1, b2) + _dot(a2, b1))
            + (_dot(a2, b2) + _dot(a1, b3_) + _dot(a3, b1)))


def _ada_kernel(c_ref, w_ref, b_ref, o_ref):
    c = c_ref[...]
    c = c * (1.0 / (1.0 + jnp.exp(-c)))
    w1, w2, w3 = _split3(w_ref[0])
    o_ref[0] = _dot_f32(c, (w1, w2, w3)) + b_ref[0]


def _ada_mod(c, ada_w, ada_b):
    depth, d, n = ada_w.shape
    bsz = c.shape[0]
    rows = 8
    cp = jnp.zeros((rows, d), f32).at[:bsz].set(c)
    tn = 512
    out = pl.pallas_call(
        _ada_kernel,
        grid=(depth, n // tn),
        in_specs=[pl.BlockSpec((rows, d), lambda l, j: (0, 0)),
                  pl.BlockSpec((1, d, tn), lambda l, j: (l, 0, j)),
                  pl.BlockSpec((1, 1, tn), lambda l, j: (l, 0, j))],
        out_specs=pl.BlockSpec((1, rows, tn), lambda l, j: (l, 0, j)),
        out_shape=jax.ShapeDtypeStruct((depth, rows, n), f32),
        compiler_params=_cparams(("parallel", "parallel")),
        name="ada_mod",
    )(cp, ada_w, ada_b.reshape(depth, 1, n))
    return out[:, :bsz]


def _rope_tables(positions, head_dim):
    rot = head_dim // 4
    half = rot // 2
    pos = positions.reshape(-1).astype(f32)
    inv_freq = ROPE_THETA ** (-jnp.arange(half, dtype=f32) / half)
    ang = pos[:, None] * inv_freq[None, :]
    cos, sin = jnp.cos(ang), jnp.sin(ang)
    lane = jnp.arange(LANES)
    q = lane % head_dim
    j = q % half
    in_rot = q < rot
    first = q < half
    c = jnp.where(in_rot[None, :], cos[:, j], 1.0)
    s1 = jnp.where(first[None, :], -sin[:, j], 0.0)
    s2 = jnp.where((in_rot & ~first)[None, :], sin[:, j], 0.0)
    return c.astype(f32), s1.astype(f32), s2.astype(f32)


def _rot_block(blk, c, s1, s2, half):
    return (blk * c + pltpu.roll(blk, LANES - half, 1) * s1
            + pltpu.roll(blk, half, 1) * s2)


def _modulate(x_ref, sc_ref, sh_ref):
    return (x_ref[...] * (1.0 + sc_ref[0]) + sh_ref[0]).astype(bf16)


def _proj_dsa_kernel(x_ref, sc_ref, sh_ref, w_ref, c128, s1_128, s2_128, c64, s1_64, s2_64,
                     lng_ref, lnb_ref, o_ref, iw_ref):
    h = _modulate(x_ref, sc_ref, sh_ref)
    acc = _dot(h, w_ref[...])
    t128 = (c128[...], s1_128[...], s2_128[...])
    t64 = (c64[...], s1_64[...], s2_64[...])
    a_scale = HEAD_DIM ** -0.5 * LOG2E
    for cb in range(4):
        blk = acc[:, cb * LANES:(cb + 1) * LANES]
        o_ref[:, cb * LANES:(cb + 1) * LANES] = (_rot_block(blk, *t128, 16) * a_scale).astype(bf16)
    for cb in range(4, 8):
        blk = acc[:, cb * LANES:(cb + 1) * LANES]
        o_ref[:, cb * LANES:(cb + 1) * LANES] = _rot_block(blk, *t64, 8).astype(bf16)
    blk = acc[:, 8 * LANES:9 * LANES]
    o_ref[:, 8 * LANES:9 * LANES] = _rot_block(blk, *t128, 16).astype(bf16)
    o_ref[:, 9 * LANES:10 * LANES] = acc[:, 9 * LANES:10 * LANES].astype(bf16)
    blk = acc[:, 10 * LANES:11 * LANES]
    lane = lax.broadcasted_iota(jnp.int32, blk.shape, 1)
    is_k = lane < IDX_DIM
    mu = jnp.sum(jnp.where(is_k, blk, 0.0), axis=1, keepdims=True) * (1.0 / IDX_DIM)
    d = jnp.where(is_k, blk - mu, 0.0)
    var = jnp.sum(d * d, axis=1, keepdims=True) * (1.0 / IDX_DIM)
    kn = d * lax.rsqrt(var + LN_EPS) * lng_ref[...] + lnb_ref[...]
    kn = jnp.where(is_k, _rot_block(kn, *t64, 8), 0.0)
    o_ref[:, 10 * LANES:11 * LANES] = kn.astype(bf16)
    o_ref[:, 11 * LANES:12 * LANES] = pltpu.roll(kn, IDX_DIM, 1).astype(bf16)
    iw_ref[...] = pltpu.roll(blk, IDX_DIM, 1) * (IDX_HEADS ** -0.5 * IDX_DIM ** -0.5)


def _proj_ssm_kernel(x_ref, sc_ref, sh_ref, w_ref, o_ref):
    h = _modulate(x_ref, sc_ref, sh_ref)
    o_ref[...] = _dot(h, w_ref[...])


def _proj_diff_kernel(x_ref, sc_ref, sh_ref, w_ref, c64, s1_64, s2_64, o_ref):
    h = _modulate(x_ref, sc_ref, sh_ref)
    acc = _dot(h, w_ref[...])
    t64 = (c64[...], s1_64[...], s2_64[...])
    q_scale = DIFF_DIM ** -0.5 * LOG2E
    for cb in range(4):
        blk = acc[:, cb * LANES:(cb + 1) * LANES]
        o_ref[:, cb * LANES:(cb + 1) * LANES] = (_rot_block(blk, *t64, 8) * q_scale).astype(bf16)
    for cb in range(4, 8):
        blk = acc[:, cb * LANES:(cb + 1) * LANES]
        o_ref[:, cb * LANES:(cb + 1) * LANES] = _rot_block(blk, *t64, 8).astype(bf16)
    o_ref[:, 8 * LANES:] = acc[:, 8 * LANES:].astype(bf16)


def _row_specs(tm, d, tiles_per_batch):
    return [pl.BlockSpec((tm, d), lambda i: (i, 0)),
            pl.BlockSpec((1, 1, d), lambda i: (i // tiles_per_batch, 0, 0)),
            pl.BlockSpec((1, 1, d), lambda i: (i // tiles_per_batch, 0, 0))]


def _tab_spec(tm):
    return pl.BlockSpec((tm, LANES), lambda i: (i, 0))


def _full_spec(shape):
    nd = len(shape)
    return pl.BlockSpec(shape, lambda i: (0,) * nd)


def _proj_dsa(x2, scale, shift, w, t128, t64, lng, lnb, seq):
    t, d = x2.shape
    tm = 256
    return pl.pallas_call(
        _proj_dsa_kernel,
        grid=(t // tm,),
        in_specs=_row_specs(tm, d, seq // tm) + [_full_spec(w.shape)] + [_tab_spec(tm)] * 6
        + [_full_spec((1, LANES))] * 2,
        out_specs=[pl.BlockSpec((tm, DSA_OUT), lambda i: (i, 0)),
                   pl.BlockSpec((tm, LANES), lambda i: (i, 0))],
        out_shape=[jax.ShapeDtypeStruct((t, DSA_OUT), bf16),
                   jax.ShapeDtypeStruct((t, LANES), f32)],
        compiler_params=_cparams(("parallel",)),
        name="proj_dsa",
    )(x2, scale, shift, w, *t128, *t64, lng, lnb)


def _proj_ssm(x2, scale, shift, w, seq):
    t, d = x2.shape
    tm = 256
    return pl.pallas_call(
        _proj_ssm_kernel,
        grid=(t // tm,),
        in_specs=_row_specs(tm, d, seq // tm) + [_full_spec(w.shape)],
        out_specs=pl.BlockSpec((tm, SSM_PAD), lambda i: (i, 0)),
        out_shape=jax.ShapeDtypeStruct((t, SSM_PAD), f32),
        compiler_params=_cparams(("parallel",)),
        name="proj_ssm",
    )(x2, scale, shift, w)


def _proj_diff(x2, scale, shift, w, t64, seq):
    t, d = x2.shape
    tm = 256
    return pl.pallas_call(
        _proj_diff_kernel,
        grid=(t // tm,),
        in_specs=_row_specs(tm, d, seq // tm) + [_full_spec(w.shape)] + [_tab_spec(tm)] * 3,
        out_specs=pl.BlockSpec((tm, DIFF_COLS), lambda i: (i, 0)),
        out_shape=jax.ShapeDtypeStruct((t, DIFF_COLS), bf16),
        compiler_params=_cparams(("parallel",)),
        name="proj_diff",
    )(x2, scale, shift, w, *t64)


def _split_w_in(w_in_l, lng, lnb):
    d = w_in_l.shape[0]
    sizes = (A_WIDTH, HEAD_DIM, HEAD_DIM, IDX_HEADS * IDX_DIM, IDX_DIM, IDX_HEADS,
             SSM_WIDTH, CONV_CH, SSM_HEADS, DIFF_WIDTH, DIFF_WIDTH, DIFF_WIDTH)
    offs = [0]
    for s in sizes:
        offs.append(offs[-1] + s)
    col = lambda k: w_in_l[:, offs[k]:offs[k + 1]]
    pad = lambda n: jnp.zeros((d, n), w_in_l.dtype)
    w_dsa = jnp.concatenate([col(0), col(3), col(1), col(2), col(4), col(5),
                             pad(LANES - IDX_DIM - IDX_HEADS)], axis=1).astype(bf16)
    w_ssm = jnp.concatenate([col(6), col(7), col(8), pad(SSM_PAD - SSM_COLS)], axis=1).astype(bf16)
    w_diff = jnp.concatenate([col(9), col(10), col(11)], axis=1).astype(bf16)
    lng_p = jnp.zeros((1, LANES), f32).at[0, :IDX_DIM].set(lng)
    lnb_p = jnp.zeros((1, LANES), f32).at[0, :IDX_DIM].set(lnb)
    return w_dsa, w_ssm, w_diff, lng_p, lnb_p


def _sortable(x):
    b = lax.bitcast_convert_type(x, jnp.int32)
    return b ^ ((b >> 31) & jnp.int32(0x7FFFFFFF))


def _online_softmax_step(s, v_blk, carry):
    m, l, acc = carry
    m_new = jnp.maximum(m, jnp.max(s, axis=1, keepdims=True))
    alpha = jnp.exp2(m - m_new)
    p = jnp.exp2(s - m_new)
    l = alpha * l + jnp.sum(p, axis=1, keepdims=True)
    acc = alpha * acc + _dot(p.astype(bf16), v_blk)
    return m_new, l, acc


def _flash_pipelined(logits_fn, v_ref, n_chunks, tk, rows):
    def process(s, idx, carry):
        off = pl.multiple_of(idx * tk, tk)
        return _online_softmax_step(s, v_ref[pl.ds(off, tk), :], carry)

    init = (jnp.full((rows, 1), NEG_BIG, f32), jnp.zeros((rows, 1), f32),
            jnp.zeros((rows, LANES), f32))
    last = n_chunks - 1
    carry = process(logits_fn(last, True), last, init)

    def pair(t, carry):
        s_a = logits_fn(2 * t, False)
        s_b = logits_fn(2 * t + 1, False)
        return process(s_b, 2 * t + 1, process(s_a, 2 * t, carry))

    def single(kc, carry):
        return process(logits_fn(kc, False), kc, carry)

    n_pairs = last // 2
    carry = lax.fori_loop(0, n_pairs, pair, carry)
    _, l, acc = lax.fori_loop(2 * n_pairs, last, single, carry)
    return acc / l


def _dsa_kernel(aq_ref, iq_ref, ak_ref, av_ref, ika_ref, ikb_ref, iw_ref, o_ref,
                keys_ref, khi_ref, klo_ref, *, tq, tk, tw, topk):
    i = pl.program_id(1)
    q_start = i * tq
    n_chunks = (q_start + tq + tk - 1) // tk
    row = q_start + lax.broadcasted_iota(jnp.int32, (tq, tk), 0)
    iw = iw_ref[...]
    w_cols = [iw[:, h:h + 1] for h in range(IDX_HEADS)]
    iq4 = jnp.concatenate([iq_ref[:, m * LANES:(m + 1) * LANES] for m in range(IDX_HEADS // 2)],
                          axis=0)

    def score_chunk(kc, carry):
        off = pl.multiple_of(kc * tk, tk)
        ka = ika_ref[pl.ds(off, tk), :]
        kb = ikb_ref[pl.ds(off, tk), :]
        ra = _dot_nt(iq4, ka)
        rb = _dot_nt(iq4, kb)
        sc = jnp.zeros((tq, tk), f32)
        for m in range(IDX_HEADS // 2):
            sc = sc + w_cols[2 * m] * jnp.maximum(ra[m * tq:(m + 1) * tq], 0.0)
            sc = sc + w_cols[2 * m + 1] * jnp.maximum(rb[m * tq:(m + 1) * tq], 0.0)
        col = off + lax.broadcasted_iota(jnp.int32, (tq, tk), 1)
        key = _sortable(jnp.where(col <= row, sc, -jnp.inf))
        keys_ref[:, pl.ds(off, tk)] = key
        khi_ref[:, pl.ds(off, tk)] = (key >> 16).astype(jnp.int16)
        klo_ref[:, pl.ds(off, tk)] = ((key & 0xFFFF) - HALF16).astype(jnp.int16)
        return carry

    lax.fori_loop(0, n_chunks, score_chunk, 0)

    n_spans = (q_start + tq + tw - 1) // tw
    lowest = jnp.full((tq, tk), -HALF16, jnp.int16)

    def fill_chunk(kc, carry):
        off = pl.multiple_of(kc * tk, tk)
        khi_ref[:, pl.ds(off, tk)] = lowest
        klo_ref[:, pl.ds(off, tk)] = lowest
        return carry

    lax.fori_loop(n_chunks, n_spans * (tw // tk), fill_chunk, 0)

    one16, zero16 = jnp.ones((), jnp.int16), jnp.zeros((), jnp.int16)

    def count_ge(ref, cand):
        c16 = cand.astype(jnp.int16)

        def body(ks, part):
            off = pl.multiple_of(ks * tw, tw)
            hit = jnp.where(ref[:, pl.ds(off, tw)] >= c16, one16, zero16)
            for c in range(tw // LANES):
                part = part + hit[:, c * LANES:(c + 1) * LANES]
            return part
        part = lax.fori_loop(0, n_spans, body, jnp.zeros((tq, LANES), jnp.int16))
        return jnp.sum(part.astype(f32), axis=1, keepdims=True)

    kf = float(topk)
    floor16 = jnp.full((tq, 1), -HALF16, jnp.int32)

    def greedy(ref, base):
        def bit_body(b, t):
            cand = t + jnp.left_shift(jnp.int32(1), 15 - b)
            return jnp.where(base + count_ge(ref, cand) >= kf, cand, t)
        return lax.fori_loop(0, 16, bit_body, floor16)

    t_hi = greedy(khi_ref, 0.0)
    n_above = count_ge(khi_ref, t_hi + 1)
    t_hi16 = t_hi.astype(jnp.int16)

    def mask_span(ks, carry):
        sl = pl.ds(pl.multiple_of(ks * tw, tw), tw)
        klo_ref[:, sl] = jnp.where(khi_ref[:, sl] == t_hi16, klo_ref[:, sl], jnp.int16(-HALF16))
        return carry

    lax.fori_loop(0, n_spans, mask_span, 0)
    t_lo = greedy(klo_ref, n_above)
    thr = jnp.maximum(t_hi * 65536 + (t_lo + HALF16), KEY_FLOOR)

    q4 = jnp.concatenate([aq_ref[:, h * LANES:(h + 1) * LANES] for h in range(A_HEADS)], axis=0)

    def logits(kc, diagonal):
        del diagonal
        off = pl.multiple_of(kc * tk, tk)
        bias = jnp.where(keys_ref[:, pl.ds(off, tk)] >= thr, 0.0, NEG_BIG)
        return _dot_nt(q4, ak_ref[pl.ds(off, tk), :]) + jnp.concatenate([bias] * A_HEADS, axis=0)

    o = _flash_pipelined(logits, av_ref, n_chunks, tk, A_HEADS * tq)
    for h in range(A_HEADS):
        o_ref[:, h * LANES:(h + 1) * LANES] = o[h * tq:(h + 1) * tq].astype(bf16)


def _dsa_attention(qkv, iw, bsz, seq):
    tq = min(128, seq)
    tk = min(512, seq)
    nq = seq // tq
    topk = min(TOPK_MAX, seq // 4)
    qspec = lambda c: pl.BlockSpec((tq, A_WIDTH), lambda b, i: (b * nq + i, c))
    kspec = lambda c: pl.BlockSpec((seq, LANES), lambda b, i: (b, c), pipeline_mode=pl.Buffered(1))
    return pl.pallas_call(
        functools.partial(_dsa_kernel, tq=tq, tk=tk, tw=min(2048, seq), topk=topk),
        grid=(bsz, nq),
        in_specs=[qspec(0), qspec(1), kspec(8), kspec(9), kspec(10), kspec(11),
                  pl.BlockSpec((tq, LANES), lambda b, i: (b * nq + i, 0))],
        out_specs=pl.BlockSpec((tq, A_WIDTH), lambda b, i: (b * nq + i, 0)),
        out_shape=jax.ShapeDtypeStruct((bsz * seq, A_WIDTH), bf16),
        scratch_shapes=[pltpu.VMEM((tq, seq), jnp.int32), pltpu.VMEM((tq, seq), jnp.int16),
                        pltpu.VMEM((tq, seq), jnp.int16)],
        compiler_params=_cparams(("parallel", "arbitrary")),
        name="dsa_attention",
    )(qkv, qkv, qkv, qkv, qkv, qkv, iw)


def _diff_kernel(q_ref, k_ref, v_ref, lam_ref, g_ref, o_ref, *, tq, tk, lam_init):
    i = pl.program_id(2)
    q_start = i * tq
    n_chunks = (q_start + tq + tk - 1) // tk
    qf = q_ref[...].astype(f32)
    lane = lax.broadcasted_iota(jnp.int32, qf.shape, 1)
    q2 = jnp.concatenate([jnp.where(lane < DIFF_DIM, qf, 0.0),
                          jnp.where(lane >= DIFF_DIM, qf, 0.0)], axis=0).astype(bf16)

    def logits(kc, diagonal):
        off = pl.multiple_of(kc * tk, tk)
        s = _dot_nt(q2, k_ref[pl.ds(off, tk), :])
        if diagonal:
            row = q_start + lax.broadcasted_iota(jnp.int32, (tq, tk), 0)
            col = off + lax.broadcasted_iota(jnp.int32, (2 * tq, tk), 1)
            s = jnp.where(col <= jnp.concatenate([row, row], axis=0), s, NEG_BIG)
        return s

    o = _flash_pipelined(logits, v_ref, n_chunks, tk, 2 * tq)
    lp = lam_ref[...]
    lam = (jnp.exp(jnp.sum(lp[0:1] * lp[1:2], axis=1, keepdims=True))
           - jnp.exp(jnp.sum(lp[2:3] * lp[3:4], axis=1, keepdims=True)) + lam_init)
    o = o[:tq] - lam * o[tq:]
    ms = jnp.mean(o * o, axis=1, keepdims=True)
    o_ref[...] = (o * lax.rsqrt(ms + RMS_EPS) * g_ref[...] * (1.0 - lam_init)).astype(bf16)


def _diff_attention(qkv, diff_lambda_l, diff_norm_g_l, lam_init, bsz, seq):
    tq = min(256, seq)
    tk = min(512, seq)
    nq = seq // tq
    return pl.pallas_call(
        functools.partial(_diff_kernel, tq=tq, tk=tk, lam_init=lam_init),
        grid=(bsz, DIFF_HEADS, nq),
        in_specs=[pl.BlockSpec((tq, LANES), lambda b, h, i: (b * nq + i, h)),
                  pl.BlockSpec((seq, LANES), lambda b, h, i: (b, DIFF_HEADS + h)),
                  pl.BlockSpec((seq, LANES), lambda b, h, i: (b, 2 * DIFF_HEADS + h)),
                  pl.BlockSpec((4, DIFF_DIM), lambda b, h, i: (0, 0)),
                  pl.BlockSpec((1, LANES), lambda b, h, i: (0, 0))],
        out_specs=pl.BlockSpec((tq, LANES), lambda b, h, i: (b * nq + i, h)),
        out_shape=jax.ShapeDtypeStruct((bsz * seq, DIFF_WIDTH), bf16),
        compiler_params=_cparams(("parallel", "parallel", "arbitrary")),
        name="diff_attention",
    )(qkv, qkv, qkv, diff_lambda_l, diff_norm_g_l.reshape(1, LANES))


def _sigmoid(x):
    return 1.0 / (1.0 + jnp.exp(-x))


def _expand_heads(v, e_ref):
    v1, v2, v3 = _split3(v)
    e = e_ref[...]
    return _dot(v1, e) + _dot(v2, e) + _dot(v3, e)


def _ssd_kernel(s_ref, cw_ref, cb_ref, dtb_ref, alog_ref, dskip_ref, ng_ref, e_ref, o_ref,
                xpad_ref, state_ref, *, q):
    hp = SSM_HEADS_PER_GROUP * SSM_HEAD_DIM
    xo, do = SSM_WIDTH, SSM_WIDTH + CONV_CH

    @pl.when(pl.program_id(1) == 0)
    def _():
        xpad_ref[0:8, :] = jnp.zeros((8, CONV_CH), f32)
        state_ref[...] = jnp.zeros(state_ref.shape, f32)

    xpad_ref[8:, :] = s_ref[:, xo:do]
    cw = cw_ref[...]
    y = cb_ref[...] + cw[3:4] * xpad_ref[8:q + 8, :]
    y = y + cw[2:3] * xpad_ref[7:q + 7, :]
    y = y + cw[1:2] * xpad_ref[6:q + 6, :]
    y = y + cw[0:1] * xpad_ref[5:q + 5, :]
    xpad_ref[0:8, :] = xpad_ref[q:q + 8, :]
    xa = y * _sigmoid(y)
    xs = xa[:, :SSM_WIDTH]

    dtr = s_ref[:, do:do + LANES] + dtb_ref[...]
    dt = jnp.maximum(dtr, 0.0) + jnp.log1p(jnp.exp(-jnp.abs(dtr)))
    a = dt * (-jnp.exp(alog_ref[...]))
    ri = lax.broadcasted_iota(jnp.int32, (q, q), 0)
    ci = lax.broadcasted_iota(jnp.int32, (q, q), 1)
    tril = ri >= ci
    tri = jnp.where(tril, 1.0, 0.0).astype(bf16)
    a1, a2, a3 = _split3(a)
    a_cum = _dot(tri, a1) + _dot(tri, a2) + _dot(tri, a3)
    a_cum_t = a_cum.T
    a_last = a_cum[q - 1:q, :]
    dt_x = _expand_heads(dt, e_ref)
    ea_x = _expand_heads(jnp.exp(a_cum), e_ref)
    dte_x = _expand_heads(jnp.exp(a_last - a_cum), e_ref)
    xdt = xs * dt_x
    xw = (xdt * dte_x).astype(bf16)
    lane = lax.broadcasted_iota(jnp.int32, (q, LANES), 1)
    lo_half = lane < SSM_HEAD_DIM

    outs = []
    for g in range(SSM_GROUPS):
        bg = xa[:, SSM_WIDTH + g * SSM_STATE:SSM_WIDTH + (g + 1) * SSM_STATE].astype(bf16)
        cg = xa[:, SSM_WIDTH + (SSM_GROUPS + g) * SSM_STATE:
                SSM_WIDTH + (SSM_GROUPS + g + 1) * SSM_STATE].astype(bf16)
        cbm = _dot_nt(cg, bg)
        pieces = []
        for pr in range(SSM_HEADS_PER_GROUP // 2):
            c0 = g * hp + pr * LANES
            xpair = xdt[:, c0:c0 + LANES]
            acc = None
            for sub in range(2):
                h = g * SSM_HEADS_PER_GROUP + 2 * pr + sub
                seg = a_cum[:, h:h + 1] - a_cum_t[h:h + 1, :]
                dec = jnp.where(tril, jnp.exp(seg), 0.0)
                mm = (cbm * dec).astype(bf16)
                keep = lo_half if sub == 0 else jnp.logical_not(lo_half)
                part = _dot(mm, jnp.where(keep, xpair, 0.0).astype(bf16))
                acc = part if acc is None else acc + part
            pieces.append(acc)
        y_diag = jnp.concatenate(pieces, axis=1)
        prev = state_ref[g]
        y_off = _dot(cg, prev.astype(bf16)) * ea_x[:, g * hp:(g + 1) * hp]
        new_states = _dot_tn(bg, xw[:, g * hp:(g + 1) * hp])
        state_ref[g] = prev * ea_x[q - 1:q, g * hp:(g + 1) * hp] + new_states
        outs.append(y_diag + y_off)
    yv = jnp.concatenate(outs, axis=1) + dskip_ref[...] * xs
    z = s_ref[:, :SSM_WIDTH]
    yv = yv * (z * _sigmoid(z))
    ng = ng_ref[...]
    for g in range(SSM_GROUPS):
        yg = yv[:, g * hp:(g + 1) * hp]
        ms = jnp.mean(yg * yg, axis=1, keepdims=True)
        o_ref[:, g * hp:(g + 1) * hp] = (yg * lax.rsqrt(ms + RMS_EPS)
                                         * ng[:, g * hp:(g + 1) * hp]).astype(bf16)


def _ssd_mixer(ssm, conv_w, conv_b, dt_bias, a_log, d_skip, norm_g, bsz, seq):
    q = math.gcd(SSD_CHUNK, seq)
    nc = seq // q
    pad16 = lambda v: jnp.zeros((1, LANES), f32).at[0, :SSM_HEADS].set(v)
    head_of_lane = jnp.arange(SSM_WIDTH) // SSM_HEAD_DIM
    expand = (jnp.arange(LANES)[:, None] == head_of_lane[None, :]).astype(bf16)
    const = lambda shape: pl.BlockSpec(shape, lambda b, c: (0,) * len(shape))
    return pl.pallas_call(
        functools.partial(_ssd_kernel, q=q),
        grid=(bsz, nc),
        in_specs=[pl.BlockSpec((q, SSM_PAD), lambda b, c: (b * nc + c, 0)),
                  const((CONV_WIDTH, CONV_CH)), const((1, CONV_CH)), const((1, LANES)),
                  const((1, LANES)), const((1, SSM_WIDTH)), const((1, SSM_WIDTH)),
                  const((LANES, SSM_WIDTH))],
        out_specs=pl.BlockSpec((q, SSM_WIDTH), lambda b, c: (b * nc + c, 0)),
        out_shape=jax.ShapeDtypeStruct((bsz * seq, SSM_WIDTH), bf16),
        scratch_shapes=[pltpu.VMEM((q + 8, CONV_CH), f32),
                        pltpu.VMEM((SSM_GROUPS, SSM_STATE, SSM_HEADS_PER_GROUP * SSM_HEAD_DIM), f32)],
        compiler_params=_cparams(("parallel", "arbitrary")),
        name="ssd_mixer",
    )(ssm, conv_w, conv_b.reshape(1, CONV_CH), pad16(dt_bias), pad16(a_log),
      jnp.repeat(d_skip, SSM_HEAD_DIM).reshape(1, SSM_WIDTH), norm_g.reshape(1, SSM_WIDTH), expand)


def _layer_norm_rows(r, g, b):
    mu = jnp.mean(r, axis=1, keepdims=True)
    d = r - mu
    var = jnp.mean(d * d, axis=1, keepdims=True)
    return d * lax.rsqrt(var + LN_EPS) * g + b


def _outproj_kernel(oa_ref, ob_ref, oc_ref, w_ref, x_ref, gate_ref, g_ref, b_ref, o_ref, *, alpha):
    y = _dot(oa_ref[...], w_ref[0:A_WIDTH, :])
    y = y + _dot(ob_ref[...], w_ref[A_WIDTH:A_WIDTH + SSM_WIDTH, :])
    y = y + _dot(oc_ref[...], w_ref[A_WIDTH + SSM_WIDTH:, :])
    r = alpha * x_ref[...] + gate_ref[0] * y
    o_ref[...] = _layer_norm_rows(r, g_ref[...], b_ref[...])


def _out_proj(oa, ob, oc, w_out_b, x2, gate, ln_g, ln_b, alpha, seq):
    t, d = x2.shape
    tm = 256
    tpb = seq // tm
    rows = lambda w: pl.BlockSpec((tm, w), lambda i: (i, 0))
    return pl.pallas_call(
        functools.partial(_outproj_kernel, alpha=alpha),
        grid=(t // tm,),
        in_specs=[rows(A_WIDTH), rows(SSM_WIDTH), rows(DIFF_WIDTH), _full_spec(w_out_b.shape),
                  rows(d), pl.BlockSpec((1, 1, d), lambda i: (i // tpb, 0, 0)),
                  _full_spec((1, d)), _full_spec((1, d))],
        out_specs=rows(d),
        out_shape=jax.ShapeDtypeStruct((t, d), f32),
        compiler_params=_cparams(("parallel",)),
        name="out_proj_ln",
    )(oa, ob, oc, w_out_b, x2, gate, ln_g.reshape(1, d), ln_b.reshape(1, d))


def _router_kernel(x_ref, sc_ref, sh_ref, w1_ref, w2_ref, w3_ref, b_ref, h_ref, idx_ref, gate_ref,
                   *, n_experts):
    h = x_ref[...] * (1.0 + sc_ref[0]) + sh_ref[0]
    h_ref[...] = h.astype(bf16)
    logits = _dot_f32(h, (w1_ref[...], w2_ref[...], w3_ref[...])) + b_ref[...]
    lane = lax.broadcasted_iota(jnp.int32, logits.shape, 1)
    lane_f = lane.astype(f32)
    cur = jnp.where(lane < n_experts, logits, -jnp.inf)
    vals, idxs = [], []
    for _ in range(TOP_EXPERTS):
        m = jnp.max(cur, axis=1, keepdims=True)
        ix = jnp.min(jnp.where(cur == m, lane_f, float(LANES)), axis=1, keepdims=True)
        vals.append(m)
        idxs.append(ix)
        cur = jnp.where(lane_f == ix, -jnp.inf, cur)
    es = [jnp.exp(v - vals[0]) for v in vals]
    tot = es[0] + es[1] + es[2] + es[3]
    gates = jnp.zeros(logits.shape, f32)
    idx_o = jnp.zeros(logits.shape, f32)
    for k in range(TOP_EXPERTS):
        gates = jnp.where(lane == k, es[k] / tot, gates)
        idx_o = jnp.where(lane == k, idxs[k], idx_o)
    gate_ref[...] = gates
    idx_ref[...] = idx_o.astype(jnp.int32)


def _router(x2, scale, shift, router_w_l, router_b_l, seq):
    t, d = x2.shape
    n_experts = router_w_l.shape[1]
    tm = 256
    wp = jnp.zeros((d, LANES), f32).at[:, :n_experts].set(router_w_l)
    w1, w2, w3 = _split3(wp)
    bp = jnp.zeros((1, LANES), f32).at[0, :n_experts].set(router_b_l)
    rows = lambda w: pl.BlockSpec((tm, w), lambda i: (i, 0))
    return pl.pallas_call(
        functools.partial(_router_kernel, n_experts=n_experts),
        grid=(t // tm,),
        in_specs=_row_specs(tm, d, seq // tm) + [_full_spec((d, LANES))] * 3 + [_full_spec((1, LANES))],
        out_specs=[rows(d), rows(LANES), rows(LANES)],
        out_shape=[jax.ShapeDtypeStruct((t, d), bf16),
                   jax.ShapeDtypeStruct((t, LANES), jnp.int32),
                   jax.ShapeDtypeStruct((t, LANES), f32)],
        compiler_params=_cparams(("parallel",)),
        name="router",
    )(x2, scale, shift, w1, w2, w3, bp)


MOE_TM = 1024
MOE_TN = 512


def _expert_up_kernel(be_ref, nb_ref, x_ref, wg_ref, wu_ref, bg_ref, bu_ref, o_ref):
    live = pl.program_id(0) < nb_ref[0]

    @pl.when(live)
    def _():
        x = x_ref[...]
        g = _dot(x, wg_ref[0].astype(bf16)) + bg_ref[0]
        u = _dot(x, wu_ref[0].astype(bf16)) + bu_ref[0]
        g = jnp.minimum(g, SWIGLU_LIMIT)
        u = jnp.clip(u, -SWIGLU_LIMIT, SWIGLU_LIMIT)
        o_ref[...] = (g * _sigmoid(SWIGLU_ALPHA * g) * (u + 1.0)).astype(bf16)

    @pl.when(jnp.logical_not(live))
    def _():
        o_ref[...] = jnp.zeros(o_ref.shape, bf16)


def _expert_down_kernel(be_ref, nb_ref, a_ref, wd_ref, bd_ref, o_ref):
    live = pl.program_id(0) < nb_ref[0]

    @pl.when(live)
    def _():
        o_ref[...] = (_dot(a_ref[...], wd_ref[0].astype(bf16)) + bd_ref[0]).astype(bf16)

    @pl.when(jnp.logical_not(live))
    def _():
        o_ref[...] = jnp.zeros(o_ref.shape, bf16)


def _expert_ffn(xs, block_expert, n_live, w_gu, b_gu, w_down, b_down, layer):
    n_slots, d = xs.shape
    depth, n_exp, _, two_f = w_gu.shape
    fdim = two_f // 2
    tm = MOE_TM
    th, tn = min(MOE_TN, fdim), min(MOE_TN, d)
    nj, nd = fdim // th, d // tn
    nb = n_slots // tm
    rowmap = lambda b, j, be, nl: (jnp.minimum(b, nl[0] - 1), 0)
    jsel = lambda b, j, nl, n: jnp.where(b < nl[0], j, n - 1)
    up_spec = pltpu.PrefetchScalarGridSpec(
        num_scalar_prefetch=2,
        grid=(nb, nj),
        in_specs=[pl.BlockSpec((tm, d), rowmap),
                  pl.BlockSpec((None, 1, d, th), lambda b, j, be, nl: (layer, be[b], 0, jsel(b, j, nl, nj))),
                  pl.BlockSpec((None, 1, d, th), lambda b, j, be, nl: (layer, be[b], 0, nj + jsel(b, j, nl, nj))),
                  pl.BlockSpec((None, 1, 1, th), lambda b, j, be, nl: (layer, be[b], 0, jsel(b, j, nl, nj))),
                  pl.BlockSpec((None, 1, 1, th), lambda b, j, be, nl: (layer, be[b], 0, nj + jsel(b, j, nl, nj)))],
        out_specs=pl.BlockSpec((tm, th), lambda b, j, be, nl: (b, j)),
    )
    b_gu4 = b_gu.reshape(depth, n_exp, 1, two_f)
    act = pl.pallas_call(
        _expert_up_kernel,
        grid_spec=up_spec,
        out_shape=jax.ShapeDtypeStruct((n_slots, fdim), bf16),
        compiler_params=_cparams(("arbitrary", "arbitrary")),
        name="expert_up",
    )(block_expert, n_live, xs, w_gu, w_gu, b_gu4, b_gu4)
    down_spec = pltpu.PrefetchScalarGridSpec(
        num_scalar_prefetch=2,
        grid=(nb, nd),
        in_specs=[pl.BlockSpec((tm, fdim), rowmap),
                  pl.BlockSpec((None, 1, fdim, tn), lambda b, j, be, nl: (layer, be[b], 0, jsel(b, j, nl, nd))),
                  pl.BlockSpec((None, 1, 1, tn), lambda b, j, be, nl: (layer, be[b], 0, jsel(b, j, nl, nd)))],
        out_specs=pl.BlockSpec((tm, tn), lambda b, j, be, nl: (b, j)),
    )
    return pl.pallas_call(
        _expert_down_kernel,
        grid_spec=down_spec,
        out_shape=jax.ShapeDtypeStruct((n_slots, d), bf16),
        compiler_params=_cparams(("arbitrary", "arbitrary")),
        name="expert_down",
    )(block_expert, n_live, act, w_down, b_down.reshape(depth, n_exp, 1, d))


def _moe_dispatch(top_idx, n_tok, n_experts):
    tm = MOE_TM
    n_assign = n_tok * TOP_EXPERTS
    flat_e = top_idx.reshape(-1)
    onehot = (flat_e[:, None] == jnp.arange(n_experts, dtype=jnp.int32)[None, :]).astype(jnp.int32)
    before = jnp.cumsum(onehot, axis=0) - onehot
    rank = jnp.sum(before * onehot, axis=1)
    counts = jnp.sum(onehot, axis=0)
    padded = ((counts + tm - 1) // tm) * tm
    pend = jnp.cumsum(padded)
    pstart = pend - padded
    dest = pstart[flat_e] + rank
    n_blocks = -(-n_assign // tm) + n_experts
    n_slots = n_blocks * tm
    flat_tok = jnp.arange(n_assign, dtype=jnp.int32) // TOP_EXPERTS
    slot_tok = jnp.full((n_slots,), n_tok, jnp.int32).at[dest].set(flat_tok)
    n_live = (pend[-1] // tm).astype(jnp.int32)
    blk_start = jnp.minimum(jnp.arange(n_blocks, dtype=jnp.int32), n_live - 1) * tm
    block_expert = jnp.sum((pend[None, :] <= blk_start[:, None]).astype(jnp.int32), axis=1)
    block_expert = jnp.minimum(block_expert, n_experts - 1)
    return dest, slot_tok, block_expert, n_live.reshape(1)


def _final_kernel(x_ref, y_ref, rg_ref, gate_ref, g_ref, b_ref, o_ref, *, alpha):
    rg = rg_ref[...]
    tm = rg.shape[0]
    shape = (tm, TOP_EXPERTS * tm)
    rank = (lax.broadcasted_iota(jnp.int32, shape, 1)
            - TOP_EXPERTS * lax.broadcasted_iota(jnp.int32, shape, 0))
    gmat = jnp.zeros(shape, f32)
    for k in range(TOP_EXPERTS):
        gmat = jnp.where(rank == k, rg[:, k:k + 1], gmat)
    g_hi = gmat.astype(bf16)
    g_lo = (gmat - g_hi.astype(f32)).astype(bf16)
    yb = y_ref[...]
    y = _dot(g_hi, yb) + _dot(g_lo, yb)
    r = alpha * x_ref[...] + gate_ref[0] * y
    o_ref[...] = _layer_norm_rows(r, g_ref[...], b_ref[...])


def _final_ln(x2, y4, rgates, gate, ln_g, ln_b, alpha, seq):
    t, d = x2.shape
    tm = 256
    tpb = seq // tm
    return pl.pallas_call(
        functools.partial(_final_kernel, alpha=alpha),
        grid=(t // tm,),
        in_specs=[pl.BlockSpec((tm, d), lambda i: (i, 0)),
                  pl.BlockSpec((TOP_EXPERTS * tm, d), lambda i: (i, 0)),
                  pl.BlockSpec((tm, LANES), lambda i: (i, 0)),
                  pl.BlockSpec((1, 1, d), lambda i: (i // tpb, 0, 0)),
                  _full_spec((1, d)), _full_spec((1, d))],
        out_specs=pl.BlockSpec((tm, d), lambda i: (i, 0)),
        out_shape=jax.ShapeDtypeStruct((t, d), f32),
        compiler_params=_cparams(("parallel",)),
        name="moe_sum_ln",
    )(x2, y4, rgates, gate, ln_g.reshape(1, d), ln_b.reshape(1, d))


def kernel(x, c, positions, w_in, w_out, idx_ln_g, idx_ln_b, conv_w, conv_b, dt_bias, a_log,
           d_skip, ssm_norm_g, diff_lambda, diff_norm_g, ada_w, ada_b, ln_g, ln_b,
           router_w, router_b, w_gu, b_gu, w_down, b_down):
    bsz, seq, d = x.shape
    depth = w_in.shape[0]
    n_tok = bsz * seq
    n_experts = router_w.shape[-1]
    alpha = (2 * depth) ** 0.25

    mod = _ada_mod(c, ada_w, ada_b).reshape(depth, bsz, 6, 1, d)
    t128 = _rope_tables(positions, HEAD_DIM)
    t64 = _rope_tables(positions, IDX_DIM)
    x2 = x.reshape(n_tok, d)

    for layer in range(depth):
        shift_m, scale_m, gate_m, shift_f, scale_f, gate_f = (mod[layer, :, k] for k in range(6))
        w_dsa, w_ssm, w_diff, lng_p, lnb_p = _split_w_in(w_in[layer], idx_ln_g[layer], idx_ln_b[layer])

        qkv_a, iw = _proj_dsa(x2, scale_m, shift_m, w_dsa, t128, t64, lng_p, lnb_p, seq)
        out_a = _dsa_attention(qkv_a, iw, bsz, seq)
        ssm = _proj_ssm(x2, scale_m, shift_m, w_ssm, seq)
        out_b = _ssd_mixer(ssm, conv_w[layer], conv_b[layer], dt_bias[layer], a_log[layer],
                           d_skip[layer], ssm_norm_g[layer], bsz, seq)
        qkv_c = _proj_diff(x2, scale_m, shift_m, w_diff, t64, seq)
        lam_init = 0.8 - 0.6 * math.exp(-0.3 * layer)
        out_c = _diff_attention(qkv_c, diff_lambda[layer], diff_norm_g[layer], lam_init, bsz, seq)
        x2 = _out_proj(out_a, out_b, out_c, w_out[layer].astype(bf16), x2, gate_m,
                       ln_g[layer, 0], ln_b[layer, 0], alpha, seq)

        hf, top_idx, gates = _router(x2, scale_f, shift_f, router_w[layer], router_b[layer], seq)
        dest, slot_tok, block_expert, n_live = _moe_dispatch(top_idx[:, :TOP_EXPERTS], n_tok, n_experts)
        xs = hf[jnp.minimum(slot_tok, n_tok - 1)]
        ys = _expert_ffn(xs, block_expert, n_live, w_gu, b_gu, w_down, b_down, layer)
        y4 = ys[dest]
        x2 = _final_ln(x2, y4, gates, gate_f, ln_g[layer, 1], ln_b[layer, 1], alpha, seq)

    return x2.reshape(bsz, seq, d)
```

```python
import functools
import math

import jax
import jax.numpy as jnp
from jax import lax
from jax.experimental import pallas as pl
from jax.experimental.pallas import tpu as pltpu

HEAD_DIM = 128
A_HEADS = 4
A_WIDTH = A_HEADS * HEAD_DIM
IDX_HEADS = 8
IDX_DIM = 64
TOPK_MAX = 256
SSM_HEADS = 16
SSM_HEAD_DIM = 64
SSM_WIDTH = SSM_HEADS * SSM_HEAD_DIM
SSM_GROUPS = 2
SSM_HEADS_PER_GROUP = SSM_HEADS // SSM_GROUPS
SSM_STATE = 128
CONV_WIDTH = 4
CONV_CH = SSM_WIDTH + 2 * SSM_GROUPS * SSM_STATE
SSD_CHUNK = 256
DIFF_HEADS = 4
DIFF_DIM = 64
DIFF_WIDTH = DIFF_HEADS * 2 * DIFF_DIM
TOP_EXPERTS = 4
SWIGLU_LIMIT = 7.0
SWIGLU_ALPHA = 1.702
ROPE_THETA = 500000.0
LN_EPS = 1e-5
RMS_EPS = 1e-6

LANES = 128
VMEM_LIMIT = 56 * 1024 * 1024
NEG_BIG = -1e30
LOG2E = 1.4426950408889634
KEY_FLOOR = -(2 ** 31) + 0x7FFFFF + 1
HALF16 = 2 ** 15

DSA_IN = 11 * LANES
DSA_OUT = 12 * LANES
SSM_COLS = SSM_WIDTH + CONV_CH + SSM_HEADS
SSM_PAD = 21 * LANES
DIFF_COLS = 3 * DIFF_WIDTH

bf16 = jnp.bfloat16
f32 = jnp.float32


def _cparams(sem, vmem=VMEM_LIMIT):
    return pltpu.CompilerParams(dimension_semantics=sem, vmem_limit_bytes=vmem)


def _split3(a):
    a1 = a.astype(bf16)
    r = a - a1.astype(f32)
    a2 = r.astype(bf16)
    a3 = (r - a2.astype(f32)).astype(bf16)
    return a1, a2, a3


def _dot(a, b):
    return jnp.dot(a, b, preferred_element_type=f32)


def _dot_nt(a, b):
    return lax.dot_general(a, b, (((1,), (1,)), ((), ())), preferred_element_type=f32)


def _dot_tn(a, b):
    return lax.dot_general(a, b, (((0,), (0,)), ((), ())), preferred_element_type=f32)


def _dot_f32(a, b3):
    a1, a2, a3 = _split3(a)
    b1, b2, b3_ = b3
    return (_dot(a1, b1) + (_dot(a1, b2) + _dot(a2, b1))
            + (_dot(a2, b2) + _dot(a1, b3_) + _dot(a3, b1)))


def _ada_kernel(c_ref, w_ref, b_ref, o_ref):
    c = c_ref[...]
    c = c * (1.0 / (1.0 + jnp.exp(-c)))
    w1, w2, w3 = _split3(w_ref[0])
    o_ref[0] = _dot_f32(c, (w1, w2, w3)) + b_ref[0]


def _ada_mod(c, ada_w, ada_b):
    depth, d, n = ada_w.shape
    bsz = c.shape[0]
    rows = 8
    cp = jnp.zeros((rows, d), f32).at[:bsz].set(c)
    tn = 512
    out = pl.pallas_call(
        _ada_kernel,
        grid=(depth, n // tn),
        in_specs=[pl.BlockSpec((rows, d), lambda l, j: (0, 0)),
                  pl.BlockSpec((1, d, tn), lambda l, j: (l, 0, j)),
                  pl.BlockSpec((1, 1, tn), lambda l, j: (l, 0, j))],
        out_specs=pl.BlockSpec((1, rows, tn), lambda l, j: (l, 0, j)),
        out_shape=jax.ShapeDtypeStruct((depth, rows, n), f32),
        compiler_params=_cparams(("parallel", "parallel")),
        name="ada_mod",
    )(cp, ada_w, ada_b.reshape(depth, 1, n))
    return out[:, :bsz]


def _rope_tables(positions, head_dim):
    rot = head_dim // 4
    half = rot // 2
    pos = positions.reshape(-1).astype(f32)
    inv_freq = ROPE_THETA ** (-jnp.arange(half, dtype=f32) / half)
    ang = pos[:, None] * inv_freq[None, :]
    cos, sin = jnp.cos(ang), jnp.sin(ang)
    lane = jnp.arange(LANES)
    q = lane % head_dim
    j = q % half
    in_rot = q < rot
    first = q < half
    c = jnp.where(in_rot[None, :], cos[:, j], 1.0)
    s1 = jnp.where(first[None, :], -sin[:, j], 0.0)
    s2 = jnp.where((in_rot & ~first)[None, :], sin[:, j], 0.0)
    return c.astype(f32), s1.astype(f32), s2.astype(f32)


def _rot_block(blk, c, s1, s2, half):
    return (blk * c + pltpu.roll(blk, LANES - half, 1) * s1
            + pltpu.roll(blk, half, 1) * s2)


def _modulate(x_ref, sc_ref, sh_ref):
    return (x_ref[...] * (1.0 + sc_ref[0]) + sh_ref[0]).astype(bf16)


def _proj_dsa_kernel(x_ref, sc_ref, sh_ref, w_ref, c128, s1_128, s2_128, c64, s1_64, s2_64,
                     lng_ref, lnb_ref, o_ref, iw_ref):
    h = _modulate(x_ref, sc_ref, sh_ref)
    acc = _dot(h, w_ref[...])
    t128 = (c128[...], s1_128[...], s2_128[...])
    t64 = (c64[...], s1_64[...], s2_64[...])
    a_scale = HEAD_DIM ** -0.5 * LOG2E
    for cb in range(4):
        blk = acc[:, cb * LANES:(cb + 1) * LANES]
        o_ref[:, cb * LANES:(cb + 1) * LANES] = (_rot_block(blk, *t128, 16) * a_scale).astype(bf16)
    for cb in range(4, 8):
        blk = acc[:, cb * LANES:(cb + 1) * LANES]
        o_ref[:, cb * LANES:(cb + 1) * LANES] = _rot_block(blk, *t64, 8).astype(bf16)
    blk = acc[:, 8 * LANES:9 * LANES]
    o_ref[:, 8 * LANES:9 * LANES] = _rot_block(blk, *t128, 16).astype(bf16)
    o_ref[:, 9 * LANES:10 * LANES] = acc[:, 9 * LANES:10 * LANES].astype(bf16)
    blk = acc[:, 10 * LANES:11 * LANES]
    lane = lax.broadcasted_iota(jnp.int32, blk.shape, 1)
    is_k = lane < IDX_DIM
    mu = jnp.sum(jnp.where(is_k, blk, 0.0), axis=1, keepdims=True) * (1.0 / IDX_DIM)
    d = jnp.where(is_k, blk - mu, 0.0)
    var = jnp.sum(d * d, axis=1, keepdims=True) * (1.0 / IDX_DIM)
    kn = d * lax.rsqrt(var + LN_EPS) * lng_ref[...] + lnb_ref[...]
    kn = jnp.where(is_k, _rot_block(kn, *t64, 8), 0.0)
    o_ref[:, 10 * LANES:11 * LANES] = kn.astype(bf16)
    o_ref[:, 11 * LANES:12 * LANES] = pltpu.roll(kn, IDX_DIM, 1).astype(bf16)
    iw_ref[...] = pltpu.roll(blk, IDX_DIM, 1) * (IDX_HEADS ** -0.5 * IDX_DIM ** -0.5)


def _proj_ssm_kernel(x_ref, sc_ref, sh_ref, w_ref, o_ref):
    h = _modulate(x_ref, sc_ref, sh_ref)
    o_ref[...] = _dot(h, w_ref[...])


def _proj_diff_kernel(x_ref, sc_ref, sh_ref, w_ref, c64, s1_64, s2_64, o_ref):
    h = _modulate(x_ref, sc_ref, sh_ref)
    acc = _dot(h, w_ref[...])
    t64 = (c64[...], s1_64[...], s2_64[...])
    q_scale = DIFF_DIM ** -0.5 * LOG2E
    for cb in range(4):
        blk = acc[:, cb * LANES:(cb + 1) * LANES]
        o_ref[:, cb * LANES:(cb + 1) * LANES] = (_rot_block(blk, *t64, 8) * q_scale).astype(bf16)
    for cb in range(4, 8):
        blk = acc[:, cb * LANES:(cb + 1) * LANES]
        o_ref[:, cb * LANES:(cb + 1) * LANES] = _rot_block(blk, *t64, 8).astype(bf16)
    o_ref[:, 8 * LANES:] = acc[:, 8 * LANES:].astype(bf16)


def _row_specs(tm, d, tiles_per_batch):
    return [pl.BlockSpec((tm, d), lambda i: (i, 0)),
            pl.BlockSpec((1, 1, d), lambda i: (i // tiles_per_batch, 0, 0)),
            pl.BlockSpec((1, 1, d), lambda i: (i // tiles_per_batch, 0, 0))]


def _tab_spec(tm):
    return pl.BlockSpec((tm, LANES), lambda i: (i, 0))


def _full_spec(shape):
    nd = len(shape)
    return pl.BlockSpec(shape, lambda i: (0,) * nd)


def _proj_dsa(x2, scale, shift, w, t128, t64, lng, lnb, seq):
    t, d = x2.shape
    tm = 256
    return pl.pallas_call(
        _proj_dsa_kernel,
        grid=(t // tm,),
        in_specs=_row_specs(tm, d, seq // tm) + [_full_spec(w.shape)] + [_tab_spec(tm)] * 6
        + [_full_spec((1, LANES))] * 2,
        out_specs=[pl.BlockSpec((tm, DSA_OUT), lambda i: (i, 0)),
                   pl.BlockSpec((tm, LANES), lambda i: (i, 0))],
        out_shape=[jax.ShapeDtypeStruct((t, DSA_OUT), bf16),
                   jax.ShapeDtypeStruct((t, LANES), f32)],
        compiler_params=_cparams(("parallel",)),
        name="proj_dsa",
    )(x2, scale, shift, w, *t128, *t64, lng, lnb)


def _proj_ssm(x2, scale, shift, w, seq):
    t, d = x2.shape
    tm = 256
    return pl.pallas_call(
        _proj_ssm_kernel,
        grid=(t // tm,),
        in_specs=_row_specs(tm, d, seq // tm) + [_full_spec(w.shape)],
        out_specs=pl.BlockSpec((tm, SSM_PAD), lambda i: (i, 0)),
        out_shape=jax.ShapeDtypeStruct((t, SSM_PAD), f32),
        compiler_params=_cparams(("parallel",)),
        name="proj_ssm",
    )(x2, scale, shift, w)


def _proj_diff(x2, scale, shift, w, t64, seq):
    t, d = x2.shape
    tm = 256
    return pl.pallas_call(
        _proj_diff_kernel,
        grid=(t // tm,),
        in_specs=_row_specs(tm, d, seq // tm) + [_full_spec(w.shape)] + [_tab_spec(tm)] * 3,
        out_specs=pl.BlockSpec((tm, DIFF_COLS), lambda i: (i, 0)),
        out_shape=jax.ShapeDtypeStruct((t, DIFF_COLS), bf16),
        compiler_params=_cparams(("parallel",)),
        name="proj_diff",
    )(x2, scale, shift, w, *t64)


def _split_w_in(w_in_l, lng, lnb):
    d = w_in_l.shape[0]
    sizes = (A_WIDTH, HEAD_DIM, HEAD_DIM, IDX_HEADS * IDX_DIM, IDX_DIM, IDX_HEADS,
             SSM_WIDTH, CONV_CH, SSM_HEADS, DIFF_WIDTH, DIFF_WIDTH, DIFF_WIDTH)
    offs = [0]
    for s in sizes:
        offs.append(offs[-1] + s)
    col = lambda k: w_in_l[:, offs[k]:offs[k + 1]]
    pad = lambda n: jnp.zeros((d, n), w_in_l.dtype)
    w_dsa = jnp.concatenate([col(0), col(3), col(1), col(2), col(4), col(5),
                             pad(LANES - IDX_DIM - IDX_HEADS)], axis=1).astype(bf16)
    w_ssm = jnp.concatenate([col(6), col(7), col(8), pad(SSM_PAD - SSM_COLS)], axis=1).astype(bf16)
    w_diff = jnp.concatenate([col(9), col(10), col(11)], axis=1).astype(bf16)
    lng_p = jnp.zeros((1, LANES), f32).at[0, :IDX_DIM].set(lng)
    lnb_p = jnp.zeros((1, LANES), f32).at[0, :IDX_DIM].set(lnb)
    return w_dsa, w_ssm, w_diff, lng_p, lnb_p


def _sortable(x):
    b = lax.bitcast_convert_type(x, jnp.int32)
    return b ^ ((b >> 31) & jnp.int32(0x7FFFFFFF))


def _online_softmax_step(s, v_blk, carry):
    m, l, acc = carry
    m_new = jnp.maximum(m, jnp.max(s, axis=1, keepdims=True))
    alpha = jnp.exp2(m - m_new)
    p = jnp.exp2(s - m_new)
    l = alpha * l + jnp.sum(p, axis=1, keepdims=True)
    acc = alpha * acc + _dot(p.astype(bf16), v_blk)
    return m_new, l, acc


def _flash_pipelined(logits_fn, v_ref, n_chunks, tk, rows):
    def process(s, idx, carry):
        off = pl.multiple_of(idx * tk, tk)
        return _online_softmax_step(s, v_ref[pl.ds(off, tk), :], carry)

    init = (jnp.full((rows, 1), NEG_BIG, f32), jnp.zeros((rows, 1), f32),
            jnp.zeros((rows, LANES), f32))
    last = n_chunks - 1
    carry = process(logits_fn(last, True), last, init)

    def pair(t, carry):
        s_a = logits_fn(2 * t, False)
        s_b = logits_fn(2 * t + 1, False)
        return process(s_b, 2 * t + 1, process(s_a, 2 * t, carry))

    def single(kc, carry):
        return process(logits_fn(kc, False), kc, carry)

    n_pairs = last // 2
    carry = lax.fori_loop(0, n_pairs, pair, carry)
    _, l, acc = lax.fori_loop(2 * n_pairs, last, single, carry)
    return acc / l


def _dsa_kernel(aq_ref, iq_ref, ak_ref, av_ref, ika_ref, ikb_ref, iw_ref, o_ref,
                keys_ref, khi_ref, klo_ref, *, tq, tk, tw, topk):
    i = pl.program_id(1)
    q_start = i * tq
    n_chunks = (q_start + tq + tk - 1) // tk
    row = q_start + lax.broadcasted_iota(jnp.int32, (tq, tk), 0)
    iw = iw_ref[...]
    w_cols = [iw[:, h:h + 1] for h in range(IDX_HEADS)]
    iq4 = jnp.concatenate([iq_ref[:, m * LANES:(m + 1) * LANES] for m in range(IDX_HEADS // 2)],
                          axis=0)

    def score_chunk(kc, carry):
        off = pl.multiple_of(kc * tk, tk)
        ka = ika_ref[pl.ds(off, tk), :]
        kb = ikb_ref[pl.ds(off, tk), :]
        ra = _dot_nt(iq4, ka)
        rb = _dot_nt(iq4, kb)
        sc = jnp.zeros((tq, tk), f32)
        for m in range(IDX_HEADS // 2):
            sc = sc + w_cols[2 * m] * jnp.maximum(ra[m * tq:(m + 1) * tq], 0.0)
            sc = sc + w_cols[2 * m + 1] * jnp.maximum(rb[m * tq:(m + 1) * tq], 0.0)
        col = off + lax.broadcasted_iota(jnp.int32, (tq, tk), 1)
        key = _sortable(jnp.where(col <= row, sc, -jnp.inf))
        keys_ref[kc] = key
        khi_ref[kc] = (key >> 16).astype(jnp.int16)
        klo_ref[kc] = ((key & 0xFFFF) - HALF16).astype(jnp.int16)
        return carry

    lax.fori_loop(0, n_chunks, score_chunk, 0)

    n_spans = (q_start + tq + tw - 1) // tw
    cps = tw // tk
    lowest = jnp.full((tq, tk), -HALF16, jnp.int16)

    def fill_chunk(kc, carry):
        khi_ref[kc] = lowest
        klo_ref[kc] = lowest
        return carry

    lax.fori_loop(n_chunks, n_spans * cps, fill_chunk, 0)

    one16, zero16 = jnp.ones((), jnp.int16), jnp.zeros((), jnp.int16)

    def count_ge(ref, cand):
        c16 = cand.astype(jnp.int16)

        def body(ks, part):
            for j in range(cps):
                hit = jnp.where(ref[ks * cps + j] >= c16, one16, zero16)
                for c in range(tk // LANES):
                    part = part + hit[:, c * LANES:(c + 1) * LANES]
            return part
        part = lax.fori_loop(0, n_spans, body, jnp.zeros((tq, LANES), jnp.int16))
        return jnp.sum(part.astype(f32), axis=1, keepdims=True)

    kf = float(topk)
    floor16 = jnp.full((tq, 1), -HALF16, jnp.int32)

    def greedy(ref, base):
        def bit_body(b, t):
            cand = t + jnp.left_shift(jnp.int32(1), 15 - b)
            return jnp.where(base + count_ge(ref, cand) >= kf, cand, t)
        return lax.fori_loop(0, 16, bit_body, floor16)

    t_hi = greedy(khi_ref, 0.0)
    n_above = count_ge(khi_ref, t_hi + 1)
    t_hi16 = t_hi.astype(jnp.int16)

    def mask_chunk(kc, carry):
        klo_ref[kc] = jnp.where(khi_ref[kc] == t_hi16, klo_ref[kc], jnp.int16(-HALF16))
        return carry

    lax.fori_loop(0, n_spans * cps, mask_chunk, 0)
    t_lo = greedy(klo_ref, n_above)
    thr = jnp.maximum(t_hi * 65536 + (t_lo + HALF16), KEY_FLOOR)

    q4 = jnp.concatenate([aq_ref[:, h * LANES:(h + 1) * LANES] for h in range(A_HEADS)], axis=0)

    def logits(kc, diagonal):
        del diagonal
        off = pl.multiple_of(kc * tk, tk)
        bias = jnp.where(keys_ref[kc] >= thr, 0.0, NEG_BIG)
        return _dot_nt(q4, ak_ref[pl.ds(off, tk), :]) + jnp.concatenate([bias] * A_HEADS, axis=0)

    o = _flash_pipelined(logits, av_ref, n_chunks, tk, A_HEADS * tq)
    for h in range(A_HEADS):
        o_ref[:, h * LANES:(h + 1) * LANES] = o[h * tq:(h + 1) * tq].astype(bf16)


def _dsa_attention(qkv, iw, bsz, seq):
    tq = min(128, seq)
    tk = min(512, seq)
    nq = seq // tq
    topk = min(TOPK_MAX, seq // 4)
    qspec = lambda c: pl.BlockSpec((tq, A_WIDTH), lambda b, i: (b * nq + i, c))
    kspec = lambda c: pl.BlockSpec((seq, LANES), lambda b, i: (b, c), pipeline_mode=pl.Buffered(1))
    return pl.pallas_call(
        functools.partial(_dsa_kernel, tq=tq, tk=tk, tw=min(2048, seq), topk=topk),
        grid=(bsz, nq),
        in_specs=[qspec(0), qspec(1), kspec(8), kspec(9), kspec(10), kspec(11),
                  pl.BlockSpec((tq, LANES), lambda b, i: (b * nq + i, 0))],
        out_specs=pl.BlockSpec((tq, A_WIDTH), lambda b, i: (b * nq + i, 0)),
        out_shape=jax.ShapeDtypeStruct((bsz * seq, A_WIDTH), bf16),
        scratch_shapes=[pltpu.VMEM((seq // tk, tq, tk), jnp.int32),
                        pltpu.VMEM((seq // tk, tq, tk), jnp.int16),
                        pltpu.VMEM((seq // tk, tq, tk), jnp.int16)],
        compiler_params=_cparams(("parallel", "arbitrary")),
        name="dsa_attention",
    )(qkv, qkv, qkv, qkv, qkv, qkv, iw)


def _diff_kernel(q_ref, k_ref, v_ref, lam_ref, g_ref, o_ref, *, tq, tk, lam_init):
    i = pl.program_id(2)
    q_start = i * tq
    n_chunks = (q_start + tq + tk - 1) // tk
    qf = q_ref[...].astype(f32)
    lane = lax.broadcasted_iota(jnp.int32, qf.shape, 1)
    q2 = jnp.concatenate([jnp.where(lane < DIFF_DIM, qf, 0.0),
                          jnp.where(lane >= DIFF_DIM, qf, 0.0)], axis=0).astype(bf16)

    def logits(kc, diagonal):
        off = pl.multiple_of(kc * tk, tk)
        s = _dot_nt(q2, k_ref[pl.ds(off, tk), :])
        if diagonal:
            row = q_start + lax.broadcasted_iota(jnp.int32, (tq, tk), 0)
            col = off + lax.broadcasted_iota(jnp.int32, (2 * tq, tk), 1)
            s = jnp.where(col <= jnp.concatenate([row, row], axis=0), s, NEG_BIG)
        return s

    o = _flash_pipelined(logits, v_ref, n_chunks, tk, 2 * tq)
    lp = lam_ref[...]
    lam = (jnp.exp(jnp.sum(lp[0:1] * lp[1:2], axis=1, keepdims=True))
           - jnp.exp(jnp.sum(lp[2:3] * lp[3:4], axis=1, keepdims=True)) + lam_init)
    o = o[:tq] - lam * o[tq:]
    ms = jnp.mean(o * o, axis=1, keepdims=True)
    o_ref[...] = (o * lax.rsqrt(ms + RMS_EPS) * g_ref[...] * (1.0 - lam_init)).astype(bf16)


def _diff_attention(qkv, diff_lambda_l, diff_norm_g_l, lam_init, bsz, seq):
    tq = min(256, seq)
    tk = min(512, seq)
    nq = seq // tq
    return pl.pallas_call(
        functools.partial(_diff_kernel, tq=tq, tk=tk, lam_init=lam_init),
        grid=(bsz, DIFF_HEADS, nq),
        in_specs=[pl.BlockSpec((tq, LANES), lambda b, h, i: (b * nq + i, h)),
                  pl.BlockSpec((seq, LANES), lambda b, h, i: (b, DIFF_HEADS + h)),
                  pl.BlockSpec((seq, LANES), lambda b, h, i: (b, 2 * DIFF_HEADS + h)),
                  pl.BlockSpec((4, DIFF_DIM), lambda b, h, i: (0, 0)),
                  pl.BlockSpec((1, LANES), lambda b, h, i: (0, 0))],
        out_specs=pl.BlockSpec((tq, LANES), lambda b, h, i: (b * nq + i, h)),
        out_shape=jax.ShapeDtypeStruct((bsz * seq, DIFF_WIDTH), bf16),
        compiler_params=_cparams(("parallel", "parallel", "arbitrary")),
        name="diff_attention",
    )(qkv, qkv, qkv, diff_lambda_l, diff_norm_g_l.reshape(1, LANES))


def _sigmoid(x):
    return 1.0 / (1.0 + jnp.exp(-x))


def _expand_heads(v, e_ref):
    v1, v2, v3 = _split3(v)
    e = e_ref[...]
    return _dot(v1, e) + _dot(v2, e) + _dot(v3, e)


def _ssd_kernel(s_ref, cw_ref, cb_ref, dtb_ref, alog_ref, dskip_ref, ng_ref, e_ref, o_ref,
                xpad_ref, state_ref, *, q):
    hp = SSM_HEADS_PER_GROUP * SSM_HEAD_DIM
    xo, do = SSM_WIDTH, SSM_WIDTH + CONV_CH

    @pl.when(pl.program_id(1) == 0)
    def _():
        xpad_ref[0:8, :] = jnp.zeros((8, CONV_CH), f32)
        state_ref[...] = jnp.zeros(state_ref.shape, f32)

    xpad_ref[8:, :] = s_ref[:, xo:do]
    cw = cw_ref[...]
    y = cb_ref[...] + cw[3:4] * xpad_ref[8:q + 8, :]
    y = y + cw[2:3] * xpad_ref[7:q + 7, :]
    y = y + cw[1:2] * xpad_ref[6:q + 6, :]
    y = y + cw[0:1] * xpad_ref[5:q + 5, :]
    xpad_ref[0:8, :] = xpad_ref[q:q + 8, :]
    xa = y * _sigmoid(y)
    xs = xa[:, :SSM_WIDTH]

    dtr = s_ref[:, do:do + LANES] + dtb_ref[...]
    dt = jnp.maximum(dtr, 0.0) + jnp.log1p(jnp.exp(-jnp.abs(dtr)))
    a = dt * (-jnp.exp(alog_ref[...]))
    ri = lax.broadcasted_iota(jnp.int32, (q, q), 0)
    ci = lax.broadcasted_iota(jnp.int32, (q, q), 1)
    tril = ri >= ci
    tri = jnp.where(tril, 1.0, 0.0).astype(bf16)
    a1, a2, a3 = _split3(a)
    a_cum = _dot(tri, a1) + _dot(tri, a2) + _dot(tri, a3)
    a_cum_t = a_cum.T
    a_last = a_cum[q - 1:q, :]
    dt_x = _expand_heads(dt, e_ref)
    ea_x = _expand_heads(jnp.exp(a_cum), e_ref)
    dte_x = _expand_heads(jnp.exp(a_last - a_cum), e_ref)
    xdt = xs * dt_x
    xw = (xdt * dte_x).astype(bf16)
    lane = lax.broadcasted_iota(jnp.int32, (q, LANES), 1)
    lo_half = lane < SSM_HEAD_DIM

    outs = []
    for g in range(SSM_GROUPS):
        bg = xa[:, SSM_WIDTH + g * SSM_STATE:SSM_WIDTH + (g + 1) * SSM_STATE].astype(bf16)
        cg = xa[:, SSM_WIDTH + (SSM_GROUPS + g) * SSM_STATE:
                SSM_WIDTH + (SSM_GROUPS + g + 1) * SSM_STATE].astype(bf16)
        cbm = _dot_nt(cg, bg)
        pieces = []
        for pr in range(SSM_HEADS_PER_GROUP // 2):
            c0 = g * hp + pr * LANES
            xpair = xdt[:, c0:c0 + LANES]
            acc = None
            for sub in range(2):
                h = g * SSM_HEADS_PER_GROUP + 2 * pr + sub
                seg = a_cum[:, h:h + 1] - a_cum_t[h:h + 1, :]
                dec = jnp.where(tril, jnp.exp(seg), 0.0)
                mm = (cbm * dec).astype(bf16)
                keep = lo_half if sub == 0 else jnp.logical_not(lo_half)
                part = _dot(mm, jnp.where(keep, xpair, 0.0).astype(bf16))
                acc = part if acc is None else acc + part
            pieces.append(acc)
        y_diag = jnp.concatenate(pieces, axis=1)
        prev = state_ref[g]
        y_off = _dot(cg, prev.astype(bf16)) * ea_x[:, g * hp:(g + 1) * hp]
        new_states = _dot_tn(bg, xw[:, g * hp:(g + 1) * hp])
        state_ref[g] = prev * ea_x[q - 1:q, g * hp:(g + 1) * hp] + new_states
        outs.append(y_diag + y_off)
    yv = jnp.concatenate(outs, axis=1) + dskip_ref[...] * xs
    z = s_ref[:, :SSM_WIDTH]
    yv = yv * (z * _sigmoid(z))
    ng = ng_ref[...]
    for g in range(SSM_GROUPS):
        yg = yv[:, g * hp:(g + 1) * hp]
        ms = jnp.mean(yg * yg, axis=1, keepdims=True)
        o_ref[:, g * hp:(g + 1) * hp] = (yg * lax.rsqrt(ms + RMS_EPS)
                                         * ng[:, g * hp:(g + 1) * hp]).astype(bf16)


def _ssd_mixer(ssm, conv_w, conv_b, dt_bias, a_log, d_skip, norm_g, bsz, seq):
    q = math.gcd(SSD_CHUNK, seq)
    nc = seq // q
    pad16 = lambda v: jnp.zeros((1, LANES), f32).at[0, :SSM_HEADS].set(v)
    head_of_lane = jnp.arange(SSM_WIDTH) // SSM_HEAD_DIM
    expand = (jnp.arange(LANES)[:, None] == head_of_lane[None, :]).astype(bf16)
    const = lambda shape: pl.BlockSpec(shape, lambda b, c: (0,) * len(shape))
    return pl.pallas_call(
        functools.partial(_ssd_kernel, q=q),
        grid=(bsz, nc),
        in_specs=[pl.BlockSpec((q, SSM_PAD), lambda b, c: (b * nc + c, 0)),
                  const((CONV_WIDTH, CONV_CH)), const((1, CONV_CH)), const((1, LANES)),
                  const((1, LANES)), const((1, SSM_WIDTH)), const((1, SSM_WIDTH)),
                  const((LANES, SSM_WIDTH))],
        out_specs=pl.BlockSpec((q, SSM_WIDTH), lambda b, c: (b * nc + c, 0)),
        out_shape=jax.ShapeDtypeStruct((bsz * seq, SSM_WIDTH), bf16),
        scratch_shapes=[pltpu.VMEM((q + 8, CONV_CH), f32),
                        pltpu.VMEM((SSM_GROUPS, SSM_STATE, SSM_HEADS_PER_GROUP * SSM_HEAD_DIM), f32)],
        compiler_params=_cparams(("parallel", "arbitrary")),
        name="ssd_mixer",
    )(ssm, conv_w, conv_b.reshape(1, CONV_CH), pad16(dt_bias), pad16(a_log),
      jnp.repeat(d_skip, SSM_HEAD_DIM).reshape(1, SSM_WIDTH), norm_g.reshape(1, SSM_WIDTH), expand)


def _layer_norm_rows(r, g, b):
    mu = jnp.mean(r, axis=1, keepdims=True)
    d = r - mu
    var = jnp.mean(d * d, axis=1, keepdims=True)
    return d * lax.rsqrt(var + LN_EPS) * g + b


def _outproj_kernel(oa_ref, ob_ref, oc_ref, w_ref, x_ref, gate_ref, g_ref, b_ref, o_ref, *, alpha):
    y = _dot(oa_ref[...], w_ref[0:A_WIDTH, :])
    y = y + _dot(ob_ref[...], w_ref[A_WIDTH:A_WIDTH + SSM_WIDTH, :])
    y = y + _dot(oc_ref[...], w_ref[A_WIDTH + SSM_WIDTH:, :])
    r = alpha * x_ref[...] + gate_ref[0] * y
    o_ref[...] = _layer_norm_rows(r, g_ref[...], b_ref[...])


def _out_proj(oa, ob, oc, w_out_b, x2, gate, ln_g, ln_b, alpha, seq):
    t, d = x2.shape
    tm = 256
    tpb = seq // tm
    rows = lambda w: pl.BlockSpec((tm, w), lambda i: (i, 0))
    return pl.pallas_call(
        functools.partial(_outproj_kernel, alpha=alpha),
        grid=(t // tm,),
        in_specs=[rows(A_WIDTH), rows(SSM_WIDTH), rows(DIFF_WIDTH), _full_spec(w_out_b.shape),
                  rows(d), pl.BlockSpec((1, 1, d), lambda i: (i // tpb, 0, 0)),
                  _full_spec((1, d)), _full_spec((1, d))],
        out_specs=rows(d),
        out_shape=jax.ShapeDtypeStruct((t, d), f32),
        compiler_params=_cparams(("parallel",)),
        name="out_proj_ln",
    )(oa, ob, oc, w_out_b, x2, gate, ln_g.reshape(1, d), ln_b.reshape(1, d))


def _router_kernel(x_ref, sc_ref, sh_ref, w1_ref, w2_ref, w3_ref, b_ref, h_ref, idx_ref, gate_ref,
                   *, n_experts):
    h = x_ref[...] * (1.0 + sc_ref[0]) + sh_ref[0]
    h_ref[...] = h.astype(bf16)
    logits = _dot_f32(h, (w1_ref[...], w2_ref[...], w3_ref[...])) + b_ref[...]
    lane = lax.broadcasted_iota(jnp.int32, logits.shape, 1)
    lane_f = lane.astype(f32)
    cur = jnp.where(lane < n_experts, logits, -jnp.inf)
    vals, idxs = [], []
    for _ in range(TOP_EXPERTS):
        m = jnp.max(cur, axis=1, keepdims=True)
        ix = jnp.min(jnp.where(cur == m, lane_f, float(LANES)), axis=1, keepdims=True)
        vals.append(m)
        idxs.append(ix)
        cur = jnp.where(lane_f == ix, -jnp.inf, cur)
    es = [jnp.exp(v - vals[0]) for v in vals]
    tot = es[0] + es[1] + es[2] + es[3]
    gates = jnp.zeros(logits.shape, f32)
    idx_o = jnp.zeros(logits.shape, f32)
    for k in range(TOP_EXPERTS):
        gates = jnp.where(lane == k, es[k] / tot, gates)
        idx_o = jnp.where(lane == k, idxs[k], idx_o)
    gate_ref[...] = gates
    idx_ref[...] = idx_o.astype(jnp.int32)


def _router(x2, scale, shift, router_w_l, router_b_l, seq):
    t, d = x2.shape
    n_experts = router_w_l.shape[1]
    tm = 256
    wp = jnp.zeros((d, LANES), f32).at[:, :n_experts].set(router_w_l)
    w1, w2, w3 = _split3(wp)
    bp = jnp.zeros((1, LANES), f32).at[0, :n_experts].set(router_b_l)
    rows = lambda w: pl.BlockSpec((tm, w), lambda i: (i, 0))
    return pl.pallas_call(
        functools.partial(_router_kernel, n_experts=n_experts),
        grid=(t // tm,),
        in_specs=_row_specs(tm, d, seq // tm) + [_full_spec((d, LANES))] * 3 + [_full_spec((1, LANES))],
        out_specs=[rows(d), rows(LANES), rows(LANES)],
        out_shape=[jax.ShapeDtypeStruct((t, d), bf16),
                   jax.ShapeDtypeStruct((t, LANES), jnp.int32),
                   jax.ShapeDtypeStruct((t, LANES), f32)],
        compiler_params=_cparams(("parallel",)),
        name="router",
    )(x2, scale, shift, w1, w2, w3, bp)


MOE_TM = 1024
MOE_TN = 512


def _expert_up_kernel(be_ref, nb_ref, x_ref, wg_ref, wu_ref, bg_ref, bu_ref, o_ref):
    live = pl.program_id(0) < nb_ref[0]

    @pl.when(live)
    def _():
        x = x_ref[...]
        g = _dot(x, wg_ref[0].astype(bf16)) + bg_ref[0]
        u = _dot(x, wu_ref[0].astype(bf16)) + bu_ref[0]
        g = jnp.minimum(g, SWIGLU_LIMIT)
        u = jnp.clip(u, -SWIGLU_LIMIT, SWIGLU_LIMIT)
        o_ref[...] = (g * _sigmoid(SWIGLU_ALPHA * g) * (u + 1.0)).astype(bf16)

    @pl.when(jnp.logical_not(live))
    def _():
        o_ref[...] = jnp.zeros(o_ref.shape, bf16)


def _expert_down_kernel(be_ref, nb_ref, a_ref, wd_ref, bd_ref, o_ref):
    live = pl.program_id(0) < nb_ref[0]

    @pl.when(live)
    def _():
        o_ref[...] = (_dot(a_ref[...], wd_ref[0].astype(bf16)) + bd_ref[0]).astype(bf16)

    @pl.when(jnp.logical_not(live))
    def _():
        o_ref[...] = jnp.zeros(o_ref.shape, bf16)


def _expert_ffn(xs, block_expert, n_live, w_gu, b_gu, w_down, b_down, layer):
    n_slots, d = xs.shape
    depth, n_exp, _, two_f = w_gu.shape
    fdim = two_f // 2
    tm = MOE_TM
    th, tn = min(MOE_TN, fdim), min(MOE_TN, d)
    nj, nd = fdim // th, d // tn
    nb = n_slots // tm
    rowmap = lambda b, j, be, nl: (jnp.minimum(b, nl[0] - 1), 0)
    jsel = lambda b, j, nl, n: jnp.where(b < nl[0], j, n - 1)
    up_spec = pltpu.PrefetchScalarGridSpec(
        num_scalar_prefetch=2,
        grid=(nb, nj),
        in_specs=[pl.BlockSpec((tm, d), rowmap),
                  pl.BlockSpec((None, 1, d, th), lambda b, j, be, nl: (layer, be[b], 0, jsel(b, j, nl, nj))),
                  pl.BlockSpec((None, 1, d, th), lambda b, j, be, nl: (layer, be[b], 0, nj + jsel(b, j, nl, nj))),
                  pl.BlockSpec((None, 1, 1, th), lambda b, j, be, nl: (layer, be[b], 0, jsel(b, j, nl, nj))),
                  pl.BlockSpec((None, 1, 1, th), lambda b, j, be, nl: (layer, be[b], 0, nj + jsel(b, j, nl, nj)))],
        out_specs=pl.BlockSpec((tm, th), lambda b, j, be, nl: (b, j)),
    )
    b_gu4 = b_gu.reshape(depth, n_exp, 1, two_f)
    act = pl.pallas_call(
        _expert_up_kernel,
        grid_spec=up_spec,
        out_shape=jax.ShapeDtypeStruct((n_slots, fdim), bf16),
        compiler_params=_cparams(("arbitrary", "arbitrary")),
        name="expert_up",
    )(block_expert, n_live, xs, w_gu, w_gu, b_gu4, b_gu4)
    down_spec = pltpu.PrefetchScalarGridSpec(
        num_scalar_prefetch=2,
        grid=(nb, nd),
        in_specs=[pl.BlockSpec((tm, fdim), rowmap),
                  pl.BlockSpec((None, 1, fdim, tn), lambda b, j, be, nl: (layer, be[b], 0, jsel(b, j, nl, nd))),
                  pl.BlockSpec((None, 1, 1, tn), lambda b, j, be, nl: (layer, be[b], 0, jsel(b, j, nl, nd)))],
        out_specs=pl.BlockSpec((tm, tn), lambda b, j, be, nl: (b, j)),
    )
    return pl.pallas_call(
        _expert_down_kernel,
        grid_spec=down_spec,
        out_shape=jax.ShapeDtypeStruct((n_slots, d), bf16),
        compiler_params=_cparams(("arbitrary", "arbitrary")),
        name="expert_down",
    )(block_expert, n_live, act, w_down, b_down.reshape(depth, n_exp, 1, d))


def _moe_dispatch(top_idx, n_tok, n_experts):
    tm = MOE_TM
    n_assign = n_tok * TOP_EXPERTS
    flat_e = top_idx.reshape(-1)
    onehot = (flat_e[:, None] == jnp.arange(n_experts, dtype=jnp.int32)[None, :]).astype(jnp.int32)
    before = jnp.cumsum(onehot, axis=0) - onehot
    rank = jnp.sum(before * onehot, axis=1)
    counts = jnp.sum(onehot, axis=0)
    padded = ((counts + tm - 1) // tm) * tm
    pend = jnp.cumsum(padded)
    pstart = pend - padded
    dest = pstart[flat_e] + rank
    n_blocks = -(-n_assign // tm) + n_experts
    n_slots = n_blocks * tm
    flat_tok = jnp.arange(n_assign, dtype=jnp.int32) // TOP_EXPERTS
    slot_tok = jnp.full((n_slots,), n_tok, jnp.int32).at[dest].set(flat_tok)
    n_live = (pend[-1] // tm).astype(jnp.int32)
    blk_start = jnp.minimum(jnp.arange(n_blocks, dtype=jnp.int32), n_live - 1) * tm
    block_expert = jnp.sum((pend[None, :] <= blk_start[:, None]).astype(jnp.int32), axis=1)
    block_expert = jnp.minimum(block_expert, n_experts - 1)
    return dest, slot_tok, block_expert, n_live.reshape(1)


def _final_kernel(x_ref, y_ref, rg_ref, gate_ref, g_ref, b_ref, o_ref, *, alpha):
    rg = rg_ref[...]
    tm = rg.shape[0]
    shape = (tm, TOP_EXPERTS * tm)
    rank = (lax.broadcasted_iota(jnp.int32, shape, 1)
            - TOP_EXPERTS * lax.broadcasted_iota(jnp.int32, shape, 0))
    gmat = jnp.zeros(shape, f32)
    for k in range(TOP_EXPERTS):
        gmat = jnp.where(rank == k, rg[:, k:k + 1], gmat)
    g_hi = gmat.astype(bf16)
    g_lo = (gmat - g_hi.astype(f32)).astype(bf16)
    yb = y_ref[...]
    y = _dot(g_hi, yb) + _dot(g_lo, yb)
    r = alpha * x_ref[...] + gate_ref[0] * y
    o_ref[...] = _layer_norm_rows(r, g_ref[...], b_ref[...])


def _final_ln(x2, y4, rgates, gate, ln_g, ln_b, alpha, seq):
    t, d = x2.shape
    tm = 256
    tpb = seq // tm
    return pl.pallas_call(
        functools.partial(_final_kernel, alpha=alpha),
        grid=(t // tm,),
        in_specs=[pl.BlockSpec((tm, d), lambda i: (i, 0)),
                  pl.BlockSpec((TOP_EXPERTS * tm, d), lambda i: (i, 0)),
                  pl.BlockSpec((tm, LANES), lambda i: (i, 0)),
                  pl.BlockSpec((1, 1, d), lambda i: (i // tpb, 0, 0)),
                  _full_spec((1, d)), _full_spec((1, d))],
        out_specs=pl.BlockSpec((tm, d), lambda i: (i, 0)),
        out_shape=jax.ShapeDtypeStruct((t, d), f32),
        compiler_params=_cparams(("parallel",)),
        name="moe_sum_ln",
    )(x2, y4, rgates, gate, ln_g.reshape(1, d), ln_b.reshape(1, d))


def kernel(x, c, positions, w_in, w_out, idx_ln_g, idx_ln_b, conv_w, conv_b, dt_bias, a_log,
           d_skip, ssm_norm_g, diff_lambda, diff_norm_g, ada_w, ada_b, ln_g, ln_b,
           router_w, router_b, w_gu, b_gu, w_down, b_down):
    bsz, seq, d = x.shape
    depth = w_in.shape[0]
    n_tok = bsz * seq
    n_experts = router_w.shape[-1]
    alpha = (2 * depth) ** 0.25

    mod = _ada_mod(c, ada_w, ada_b).reshape(depth, bsz, 6, 1, d)
    t128 = _rope_tables(positions, HEAD_DIM)
    t64 = _rope_tables(positions, IDX_DIM)
    x2 = x.reshape(n_tok, d)

    for layer in range(depth):
        shift_m, scale_m, gate_m, shift_f, scale_f, gate_f = (mod[layer, :, k] for k in range(6))
        w_dsa, w_ssm, w_diff, lng_p, lnb_p = _split_w_in(w_in[layer], idx_ln_g[layer], idx_ln_b[layer])

        qkv_a, iw = _proj_dsa(x2, scale_m, shift_m, w_dsa, t128, t64, lng_p, lnb_p, seq)
        out_a = _dsa_attention(qkv_a, iw, bsz, seq)
        ssm = _proj_ssm(x2, scale_m, shift_m, w_ssm, seq)
        out_b = _ssd_mixer(ssm, conv_w[layer], conv_b[layer], dt_bias[layer], a_log[layer],
                           d_skip[layer], ssm_norm_g[layer], bsz, seq)
        qkv_c = _proj_diff(x2, scale_m, shift_m, w_diff, t64, seq)
        lam_init = 0.8 - 0.6 * math.exp(-0.3 * layer)
        out_c = _diff_attention(qkv_c, diff_lambda[layer], diff_norm_g[layer], lam_init, bsz, seq)
        x2 = _out_proj(out_a, out_b, out_c, w_out[layer].astype(bf16), x2, gate_m,
                       ln_g[layer, 0], ln_b[layer, 0], alpha, seq)

        hf, top_idx, gates = _router(x2, scale_f, shift_f, router_w[layer], router_b[layer], seq)
        dest, slot_tok, block_expert, n_live = _moe_dispatch(top_idx[:, :TOP_EXPERTS], n_tok, n_experts)
        xs = hf[jnp.minimum(slot_tok, n_tok - 1)]
        ys = _expert_ffn(xs, block_expert, n_live, w_gu, b_gu, w_down, b_down, layer)
        y4 = ys[dest]
        x2 = _final_ln(x2, y4, gates, gate_f, ln_g[layer, 1], ln_b[layer, 1], alpha, seq)

    return x2.reshape(bsz, seq, d)
```

```python
import functools
import math

import jax
import jax.numpy as jnp
from jax import lax
from jax.experimental import pallas as pl
from jax.experimental.pallas import tpu as pltpu

HEAD_DIM = 128
A_HEADS = 4
A_WIDTH = A_HEADS * HEAD_DIM
IDX_HEADS = 8
IDX_DIM = 64
TOPK_MAX = 256
SSM_HEADS = 16
SSM_HEAD_DIM = 64
SSM_WIDTH = SSM_HEADS * SSM_HEAD_DIM
SSM_GROUPS = 2
SSM_HEADS_PER_GROUP = SSM_HEADS // SSM_GROUPS
SSM_STATE = 128
CONV_WIDTH = 4
CONV_CH = SSM_WIDTH + 2 * SSM_GROUPS * SSM_STATE
SSD_CHUNK = 256
DIFF_HEADS = 4
DIFF_DIM = 64
DIFF_WIDTH = DIFF_HEADS * 2 * DIFF_DIM
TOP_EXPERTS = 4
SWIGLU_LIMIT = 7.0
SWIGLU_ALPHA = 1.702
ROPE_THETA = 500000.0
LN_EPS = 1e-5
RMS_EPS = 1e-6

LANES = 128
VMEM_LIMIT = 56 * 1024 * 1024
NEG_BIG = -1e30
LOG2E = 1.4426950408889634
KEY_FLOOR = -(2 ** 31) + 0x7FFFFF + 1
HALF16 = 2 ** 15

DSA_IN = 11 * LANES
DSA_OUT = 12 * LANES
SSM_COLS = SSM_WIDTH + CONV_CH + SSM_HEADS
SSM_PAD = 21 * LANES
DIFF_COLS = 3 * DIFF_WIDTH

bf16 = jnp.bfloat16
f32 = jnp.float32


def _cparams(sem, vmem=VMEM_LIMIT):
    return pltpu.CompilerParams(dimension_semantics=sem, vmem_limit_bytes=vmem)


def _split3(a):
    a1 = a.astype(bf16)
    r = a - a1.astype(f32)
    a2 = r.astype(bf16)
    a3 = (r - a2.astype(f32)).astype(bf16)
    return a1, a2, a3


def _dot(a, b):
    return jnp.dot(a, b, preferred_element_type=f32)


def _dot_nt(a, b):
    return lax.dot_general(a, b, (((1,), (1,)), ((), ())), preferred_element_type=f32)


def _dot_tn(a, b):
    return lax.dot_general(a, b, (((0,), (0,)), ((), ())), preferred_element_type=f32)


def _dot_f32(a, b3):
    a1, a2, a3 = _split3(a)
    b1, b2, b3_ = b3
    return (_dot(a1, b1) + (_dot(a1, b2) + _dot(a2, b1))
            + (_dot(a2, b2) + _dot(a1, b3_) + _dot(a3, b1)))


def _ada_kernel(c_ref, w_ref, b_ref, o_ref):
    c = c_ref[...]
    c = c * (1.0 / (1.0 + jnp.exp(-c)))
    w1, w2, w3 = _split3(w_ref[0])
    o_ref[0] = _dot_f32(c, (w1, w2, w3)) + b_ref[0]


def _ada_mod(c, ada_w, ada_b):
    depth, d, n = ada_w.shape
    bsz = c.shape[0]
    rows = 8
    cp = jnp.zeros((rows, d), f32).at[:bsz].set(c)
    tn = 512
    out = pl.pallas_call(
        _ada_kernel,
        grid=(depth, n // tn),
        in_specs=[pl.BlockSpec((rows, d), lambda l, j: (0, 0)),
                  pl.BlockSpec((1, d, tn), lambda l, j: (l, 0, j)),
                  pl.BlockSpec((1, 1, tn), lambda l, j: (l, 0, j))],
        out_specs=pl.BlockSpec((1, rows, tn), lambda l, j: (l, 0, j)),
        out_shape=jax.ShapeDtypeStruct((depth, rows, n), f32),
        compiler_params=_cparams(("parallel", "parallel")),
        name="ada_mod",
    )(cp, ada_w, ada_b.reshape(depth, 1, n))
    return out[:, :bsz]


def _rope_tables(positions, head_dim):
    rot = head_dim // 4
    half = rot // 2
    pos = positions.reshape(-1).astype(f32)
    inv_freq = ROPE_THETA ** (-jnp.arange(half, dtype=f32) / half)
    ang = pos[:, None] * inv_freq[None, :]
    cos, sin = jnp.cos(ang), jnp.sin(ang)
    lane = jnp.arange(LANES)
    q = lane % head_dim
    j = q % half
    in_rot = q < rot
    first = q < half
    c = jnp.where(in_rot[None, :], cos[:, j], 1.0)
    s1 = jnp.where(first[None, :], -sin[:, j], 0.0)
    s2 = jnp.where((in_rot & ~first)[None, :], sin[:, j], 0.0)
    return c.astype(f32), s1.astype(f32), s2.astype(f32)


def _rot_block(blk, c, s1, s2, half):
    return (blk * c + pltpu.roll(blk, LANES - half, 1) * s1
            + pltpu.roll(blk, half, 1) * s2)


def _modulate(x_ref, sc_ref, sh_ref):
    return (x_ref[...] * (1.0 + sc_ref[0]) + sh_ref[0]).astype(bf16)


def _proj_dsa_kernel(x_ref, sc_ref, sh_ref, w_ref, c128, s1_128, s2_128, c64, s1_64, s2_64,
                     lng_ref, lnb_ref, o_ref, iw_ref):
    h = _modulate(x_ref, sc_ref, sh_ref)
    acc = _dot(h, w_ref[...])
    t128 = (c128[...], s1_128[...], s2_128[...])
    t64 = (c64[...], s1_64[...], s2_64[...])
    a_scale = HEAD_DIM ** -0.5 * LOG2E
    for cb in range(4):
        blk = acc[:, cb * LANES:(cb + 1) * LANES]
        o_ref[:, cb * LANES:(cb + 1) * LANES] = (_rot_block(blk, *t128, 16) * a_scale).astype(bf16)
    for cb in range(4, 8):
        blk = acc[:, cb * LANES:(cb + 1) * LANES]
        o_ref[:, cb * LANES:(cb + 1) * LANES] = _rot_block(blk, *t64, 8).astype(bf16)
    blk = acc[:, 8 * LANES:9 * LANES]
    o_ref[:, 8 * LANES:9 * LANES] = _rot_block(blk, *t128, 16).astype(bf16)
    o_ref[:, 9 * LANES:10 * LANES] = acc[:, 9 * LANES:10 * LANES].astype(bf16)
    blk = acc[:, 10 * LANES:11 * LANES]
    lane = lax.broadcasted_iota(jnp.int32, blk.shape, 1)
    is_k = lane < IDX_DIM
    mu = jnp.sum(jnp.where(is_k, blk, 0.0), axis=1, keepdims=True) * (1.0 / IDX_DIM)
    d = jnp.where(is_k, blk - mu, 0.0)
    var = jnp.sum(d * d, axis=1, keepdims=True) * (1.0 / IDX_DIM)
    kn = d * lax.rsqrt(var + LN_EPS) * lng_ref[...] + lnb_ref[...]
    kn = jnp.where(is_k, _rot_block(kn, *t64, 8), 0.0)
    o_ref[:, 10 * LANES:11 * LANES] = kn.astype(bf16)
    o_ref[:, 11 * LANES:12 * LANES] = pltpu.roll(kn, IDX_DIM, 1).astype(bf16)
    iw_ref[...] = pltpu.roll(blk, IDX_DIM, 1) * (IDX_HEADS ** -0.5 * IDX_DIM ** -0.5)


def _proj_ssm_kernel(x_ref, sc_ref, sh_ref, w_ref, o_ref):
    h = _modulate(x_ref, sc_ref, sh_ref)
    o_ref[...] = _dot(h, w_ref[...])


def _proj_diff_kernel(x_ref, sc_ref, sh_ref, w_ref, c64, s1_64, s2_64, o_ref):
    h = _modulate(x_ref, sc_ref, sh_ref)
    acc = _dot(h, w_ref[...])
    t64 = (c64[...], s1_64[...], s2_64[...])
    q_scale = DIFF_DIM ** -0.5 * LOG2E
    for cb in range(4):
        blk = acc[:, cb * LANES:(cb + 1) * LANES]
        o_ref[:, cb * LANES:(cb + 1) * LANES] = (_rot_block(blk, *t64, 8) * q_scale).astype(bf16)
    for cb in range(4, 8):
        blk = acc[:, cb * LANES:(cb + 1) * LANES]
        o_ref[:, cb * LANES:(cb + 1) * LANES] = _rot_block(blk, *t64, 8).astype(bf16)
    o_ref[:, 8 * LANES:] = acc[:, 8 * LANES:].astype(bf16)


def _row_specs(tm, d, tiles_per_batch):
    return [pl.BlockSpec((tm, d), lambda i: (i, 0)),
            pl.BlockSpec((1, 1, d), lambda i: (i // tiles_per_batch, 0, 0)),
            pl.BlockSpec((1, 1, d), lambda i: (i // tiles_per_batch, 0, 0))]


def _tab_spec(tm):
    return pl.BlockSpec((tm, LANES), lambda i: (i, 0))


def _full_spec(shape):
    nd = len(shape)
    return pl.BlockSpec(shape, lambda i: (0,) * nd)


def _proj_dsa(x2, scale, shift, w, t128, t64, lng, lnb, seq):
    t, d = x2.shape
    tm = 256
    return pl.pallas_call(
        _proj_dsa_kernel,
        grid=(t // tm,),
        in_specs=_row_specs(tm, d, seq // tm) + [_full_spec(w.shape)] + [_tab_spec(tm)] * 6
        + [_full_spec((1, LANES))] * 2,
        out_specs=[pl.BlockSpec((tm, DSA_OUT), lambda i: (i, 0)),
                   pl.BlockSpec((tm, LANES), lambda i: (i, 0))],
        out_shape=[jax.ShapeDtypeStruct((t, DSA_OUT), bf16),
                   jax.ShapeDtypeStruct((t, LANES), f32)],
        compiler_params=_cparams(("parallel",)),
        name="proj_dsa",
    )(x2, scale, shift, w, *t128, *t64, lng, lnb)


def _proj_ssm(x2, scale, shift, w, seq):
    t, d = x2.shape
    tm = 256
    return pl.pallas_call(
        _proj_ssm_kernel,
        grid=(t // tm,),
        in_specs=_row_specs(tm, d, seq // tm) + [_full_spec(w.shape)],
        out_specs=pl.BlockSpec((tm, SSM_PAD), lambda i: (i, 0)),
        out_shape=jax.ShapeDtypeStruct((t, SSM_PAD), f32),
        compiler_params=_cparams(("parallel",)),
        name="proj_ssm",
    )(x2, scale, shift, w)


def _proj_diff(x2, scale, shift, w, t64, seq):
    t, d = x2.shape
    tm = 256
    return pl.pallas_call(
        _proj_diff_kernel,
        grid=(t // tm,),
        in_specs=_row_specs(tm, d, seq // tm) + [_full_spec(w.shape)] + [_tab_spec(tm)] * 3,
        out_specs=pl.BlockSpec((tm, DIFF_COLS), lambda i: (i, 0)),
        out_shape=jax.ShapeDtypeStruct((t, DIFF_COLS), bf16),
        compiler_params=_cparams(("parallel",)),
        name="proj_diff",
    )(x2, scale, shift, w, *t64)


def _split_w_in(w_in_l, lng, lnb):
    d = w_in_l.shape[0]
    sizes = (A_WIDTH, HEAD_DIM, HEAD_DIM, IDX_HEADS * IDX_DIM, IDX_DIM, IDX_HEADS,
             SSM_WIDTH, CONV_CH, SSM_HEADS, DIFF_WIDTH, DIFF_WIDTH, DIFF_WIDTH)
    offs = [0]
    for s in sizes:
        offs.append(offs[-1] + s)
    col = lambda k: w_in_l[:, offs[k]:offs[k + 1]]
    pad = lambda n: jnp.zeros((d, n), w_in_l.dtype)
    w_dsa = jnp.concatenate([col(0), col(3), col(1), col(2), col(4), col(5),
                             pad(LANES - IDX_DIM - IDX_HEADS)], axis=1).astype(bf16)
    w_ssm = jnp.concatenate([col(6), col(7), col(8), pad(SSM_PAD - SSM_COLS)], axis=1).astype(bf16)
    w_diff = jnp.concatenate([col(9), col(10), col(11)], axis=1).astype(bf16)
    lng_p = jnp.zeros((1, LANES), f32).at[0, :IDX_DIM].set(lng)
    lnb_p = jnp.zeros((1, LANES), f32).at[0, :IDX_DIM].set(lnb)
    return w_dsa, w_ssm, w_diff, lng_p, lnb_p


def _sortable(x):
    b = lax.bitcast_convert_type(x, jnp.int32)
    return b ^ ((b >> 31) & jnp.int32(0x7FFFFFFF))


def _online_softmax_step(s, v_blk, carry):
    m, l, acc = carry
    m_new = jnp.maximum(m, jnp.max(s, axis=1, keepdims=True))
    alpha = jnp.exp2(m - m_new)
    p = jnp.exp2(s - m_new)
    l = alpha * l + jnp.sum(p, axis=1, keepdims=True)
    acc = alpha * acc + _dot(p.astype(bf16), v_blk)
    return m_new, l, acc


def _flash_pipelined(logits_fn, v_ref, n_chunks, tk, rows):
    def process(s, idx, carry):
        off = pl.multiple_of(idx * tk, tk)
        return _online_softmax_step(s, v_ref[pl.ds(off, tk), :], carry)

    init = (jnp.full((rows, 1), NEG_BIG, f32), jnp.zeros((rows, 1), f32),
            jnp.zeros((rows, LANES), f32))
    last = n_chunks - 1
    carry = process(logits_fn(last, True), last, init)

    def pair(t, carry):
        s_a = logits_fn(2 * t, False)
        s_b = logits_fn(2 * t + 1, False)
        return process(s_b, 2 * t + 1, process(s_a, 2 * t, carry))

    def single(kc, carry):
        return process(logits_fn(kc, False), kc, carry)

    n_pairs = last // 2
    carry = lax.fori_loop(0, n_pairs, pair, carry)
    _, l, acc = lax.fori_loop(2 * n_pairs, last, single, carry)
    return acc / l


def _dsa_kernel(aq_ref, iq_ref, ak_ref, av_ref, ika_ref, ikb_ref, iw_ref, o_ref,
                keys_ref, khi_ref, klo_ref, *, tq, tk, tw, topk):
    i = pl.program_id(1)
    q_start = i * tq
    n_chunks = (q_start + tq + tk - 1) // tk
    row = q_start + lax.broadcasted_iota(jnp.int32, (tq, tk), 0)
    iw = iw_ref[...]
    w_tiles = [jnp.broadcast_to(iw[:, h:h + 1], (tq, LANES)) for h in range(IDX_HEADS)]
    iq4 = jnp.concatenate([iq_ref[:, m * LANES:(m + 1) * LANES] for m in range(IDX_HEADS // 2)],
                          axis=0)

    def score_chunk(kc, carry):
        off = pl.multiple_of(kc * tk, tk)
        ka = ika_ref[pl.ds(off, tk), :]
        kb = ikb_ref[pl.ds(off, tk), :]
        ra = _dot_nt(iq4, ka)
        rb = _dot_nt(iq4, kb)
        cols = []
        for c in range(tk // LANES):
            lanes = slice(c * LANES, (c + 1) * LANES)
            acc = jnp.zeros((tq, LANES), f32)
            for m in range(IDX_HEADS // 2):
                acc = acc + w_tiles[2 * m] * jnp.maximum(ra[m * tq:(m + 1) * tq, lanes], 0.0)
                acc = acc + w_tiles[2 * m + 1] * jnp.maximum(rb[m * tq:(m + 1) * tq, lanes], 0.0)
            cols.append(acc)
        sc = jnp.concatenate(cols, axis=1)
        col = off + lax.broadcasted_iota(jnp.int32, (tq, tk), 1)
        key = _sortable(jnp.where(col <= row, sc, -jnp.inf))
        keys_ref[kc] = key
        khi_ref[kc] = (key >> 16).astype(jnp.int16)
        klo_ref[kc] = ((key & 0xFFFF) - HALF16).astype(jnp.int16)
        return carry

    lax.fori_loop(0, n_chunks, score_chunk, 0)

    n_spans = (q_start + tq + tw - 1) // tw
    cps = tw // tk
    lowest = jnp.full((tq, tk), -HALF16, jnp.int16)

    def fill_chunk(kc, carry):
        khi_ref[kc] = lowest
        klo_ref[kc] = lowest
        return carry

    lax.fori_loop(n_chunks, n_spans * cps, fill_chunk, 0)

    one16, zero16 = jnp.ones((), jnp.int16), jnp.zeros((), jnp.int16)

    def count_ge(ref, cand):
        c16 = cand.astype(jnp.int16)

        def body(ks, part):
            for j in range(cps):
                hit = jnp.where(ref[ks * cps + j] >= c16, one16, zero16)
                for c in range(tk // LANES):
                    part = part + hit[:, c * LANES:(c + 1) * LANES]
            return part
        part = lax.fori_loop(0, n_spans, body, jnp.zeros((tq, LANES), jnp.int16))
        return jnp.sum(part.astype(f32), axis=1, keepdims=True)

    kf = float(topk)
    floor16 = jnp.full((tq, 1), -HALF16, jnp.int32)

    def greedy(ref, base):
        def bit_body(b, t):
            cand = t + jnp.left_shift(jnp.int32(1), 15 - b)
            return jnp.where(base + count_ge(ref, cand) >= kf, cand, t)
        return lax.fori_loop(0, 16, bit_body, floor16)

    t_hi = greedy(khi_ref, 0.0)
    n_above = count_ge(khi_ref, t_hi + 1)
    t_hi16 = t_hi.astype(jnp.int16)

    def mask_chunk(kc, carry):
        klo_ref[kc] = jnp.where(khi_ref[kc] == t_hi16, klo_ref[kc], jnp.int16(-HALF16))
        return carry

    lax.fori_loop(0, n_spans * cps, mask_chunk, 0)
    t_lo = greedy(klo_ref, n_above)
    thr = jnp.maximum(t_hi * 65536 + (t_lo + HALF16), KEY_FLOOR)
    thr_tile = jnp.broadcast_to(thr, (tq, LANES))

    q4 = jnp.concatenate([aq_ref[:, h * LANES:(h + 1) * LANES] for h in range(A_HEADS)], axis=0)

    def logits(kc, diagonal):
        del diagonal
        off = pl.multiple_of(kc * tk, tk)
        keys = keys_ref[kc]
        bias = jnp.concatenate(
            [jnp.where(keys[:, c * LANES:(c + 1) * LANES] >= thr_tile, 0.0, NEG_BIG)
             for c in range(tk // LANES)], axis=1)
        return _dot_nt(q4, ak_ref[pl.ds(off, tk), :]) + jnp.concatenate([bias] * A_HEADS, axis=0)

    o = _flash_pipelined(logits, av_ref, n_chunks, tk, A_HEADS * tq)
    for h in range(A_HEADS):
        o_ref[:, h * LANES:(h + 1) * LANES] = o[h * tq:(h + 1) * tq].astype(bf16)


def _dsa_attention(qkv, iw, bsz, seq):
    tq = min(128, seq)
    tk = min(512, seq)
    nq = seq // tq
    topk = min(TOPK_MAX, seq // 4)
    qspec = lambda c: pl.BlockSpec((tq, A_WIDTH), lambda b, i: (b * nq + i, c))
    kspec = lambda c: pl.BlockSpec((seq, LANES), lambda b, i: (b, c), pipeline_mode=pl.Buffered(1))
    return pl.pallas_call(
        functools.partial(_dsa_kernel, tq=tq, tk=tk, tw=min(2048, seq), topk=topk),
        grid=(bsz, nq),
        in_specs=[qspec(0), qspec(1), kspec(8), kspec(9), kspec(10), kspec(11),
                  pl.BlockSpec((tq, LANES), lambda b, i: (b * nq + i, 0))],
        out_specs=pl.BlockSpec((tq, A_WIDTH), lambda b, i: (b * nq + i, 0)),
        out_shape=jax.ShapeDtypeStruct((bsz * seq, A_WIDTH), bf16),
        scratch_shapes=[pltpu.VMEM((seq // tk, tq, tk), jnp.int32),
                        pltpu.VMEM((seq // tk, tq, tk), jnp.int16),
                        pltpu.VMEM((seq // tk, tq, tk), jnp.int16)],
        compiler_params=_cparams(("parallel", "arbitrary")),
        name="dsa_attention",
    )(qkv, qkv, qkv, qkv, qkv, qkv, iw)


def _diff_kernel(q_ref, k_ref, v_ref, lam_ref, g_ref, o_ref, *, tq, tk, lam_init):
    i = pl.program_id(2)
    q_start = i * tq
    n_chunks = (q_start + tq + tk - 1) // tk
    qf = q_ref[...].astype(f32)
    lane = lax.broadcasted_iota(jnp.int32, qf.shape, 1)
    q2 = jnp.concatenate([jnp.where(lane < DIFF_DIM, qf, 0.0),
                          jnp.where(lane >= DIFF_DIM, qf, 0.0)], axis=0).astype(bf16)

    def logits(kc, diagonal):
        off = pl.multiple_of(kc * tk, tk)
        s = _dot_nt(q2, k_ref[pl.ds(off, tk), :])
        if diagonal:
            row = q_start + lax.broadcasted_iota(jnp.int32, (tq, tk), 0)
            col = off + lax.broadcasted_iota(jnp.int32, (2 * tq, tk), 1)
            s = jnp.where(col <= jnp.concatenate([row, row], axis=0), s, NEG_BIG)
        return s

    o = _flash_pipelined(logits, v_ref, n_chunks, tk, 2 * tq)
    lp = lam_ref[...]
    lam = (jnp.exp(jnp.sum(lp[0:1] * lp[1:2], axis=1, keepdims=True))
           - jnp.exp(jnp.sum(lp[2:3] * lp[3:4], axis=1, keepdims=True)) + lam_init)
    o = o[:tq] - lam * o[tq:]
    ms = jnp.mean(o * o, axis=1, keepdims=True)
    o_ref[...] = (o * lax.rsqrt(ms + RMS_EPS) * g_ref[...] * (1.0 - lam_init)).astype(bf16)


def _diff_attention(qkv, diff_lambda_l, diff_norm_g_l, lam_init, bsz, seq):
    tq = min(256, seq)
    tk = min(512, seq)
    nq = seq // tq
    return pl.pallas_call(
        functools.partial(_diff_kernel, tq=tq, tk=tk, lam_init=lam_init),
        grid=(bsz, DIFF_HEADS, nq),
        in_specs=[pl.BlockSpec((tq, LANES), lambda b, h, i: (b * nq + i, h)),
                  pl.BlockSpec((seq, LANES), lambda b, h, i: (b, DIFF_HEADS + h)),
                  pl.BlockSpec((seq, LANES), lambda b, h, i: (b, 2 * DIFF_HEADS + h)),
                  pl.BlockSpec((4, DIFF_DIM), lambda b, h, i: (0, 0)),
                  pl.BlockSpec((1, LANES), lambda b, h, i: (0, 0))],
        out_specs=pl.BlockSpec((tq, LANES), lambda b, h, i: (b * nq + i, h)),
        out_shape=jax.ShapeDtypeStruct((bsz * seq, DIFF_WIDTH), bf16),
        compiler_params=_cparams(("parallel", "parallel", "arbitrary")),
        name="diff_attention",
    )(qkv, qkv, qkv, diff_lambda_l, diff_norm_g_l.reshape(1, LANES))


def _sigmoid(x):
    return 1.0 / (1.0 + jnp.exp(-x))


def _expand_heads(v, e_ref):
    v1, v2, v3 = _split3(v)
    e = e_ref[...]
    return _dot(v1, e) + _dot(v2, e) + _dot(v3, e)


def _ssd_kernel(s_ref, cw_ref, cb_ref, dtb_ref, alog_ref, dskip_ref, ng_ref, e_ref, o_ref,
                xpad_ref, state_ref, *, q):
    hp = SSM_HEADS_PER_GROUP * SSM_HEAD_DIM
    xo, do = SSM_WIDTH, SSM_WIDTH + CONV_CH

    @pl.when(pl.program_id(1) == 0)
    def _():
        xpad_ref[0:8, :] = jnp.zeros((8, CONV_CH), f32)
        state_ref[...] = jnp.zeros(state_ref.shape, f32)

    xpad_ref[8:, :] = s_ref[:, xo:do]
    cw = cw_ref[...]
    y = cb_ref[...] + cw[3:4] * xpad_ref[8:q + 8, :]
    y = y + cw[2:3] * xpad_ref[7:q + 7, :]
    y = y + cw[1:2] * xpad_ref[6:q + 6, :]
    y = y + cw[0:1] * xpad_ref[5:q + 5, :]
    xpad_ref[0:8, :] = xpad_ref[q:q + 8, :]
    xa = y * _sigmoid(y)
    xs = xa[:, :SSM_WIDTH]

    dtr = s_ref[:, do:do + LANES] + dtb_ref[...]
    dt = jnp.maximum(dtr, 0.0) + jnp.log1p(jnp.exp(-jnp.abs(dtr)))
    a = dt * (-jnp.exp(alog_ref[...]))
    ri = lax.broadcasted_iota(jnp.int32, (q, q), 0)
    ci = lax.broadcasted_iota(jnp.int32, (q, q), 1)
    tril = ri >= ci
    tri = jnp.where(tril, 1.0, 0.0).astype(bf16)
    a1, a2, a3 = _split3(a)
    a_cum = _dot(tri, a1) + _dot(tri, a2) + _dot(tri, a3)
    a_cum_t = a_cum.T
    a_last = a_cum[q - 1:q, :]
    dt_x = _expand_heads(dt, e_ref)
    ea_x = _expand_heads(jnp.exp(a_cum), e_ref)
    dte_x = _expand_heads(jnp.exp(a_last - a_cum), e_ref)
    xdt = xs * dt_x
    xw = (xdt * dte_x).astype(bf16)
    lane = lax.broadcasted_iota(jnp.int32, (q, LANES), 1)
    lo_half = lane < SSM_HEAD_DIM

    outs = []
    for g in range(SSM_GROUPS):
        bg = xa[:, SSM_WIDTH + g * SSM_STATE:SSM_WIDTH + (g + 1) * SSM_STATE].astype(bf16)
        cg = xa[:, SSM_WIDTH + (SSM_GROUPS + g) * SSM_STATE:
                SSM_WIDTH + (SSM_GROUPS + g + 1) * SSM_STATE].astype(bf16)
        cbm = _dot_nt(cg, bg)
        pieces = []
        for pr in range(SSM_HEADS_PER_GROUP // 2):
            c0 = g * hp + pr * LANES
            xpair = xdt[:, c0:c0 + LANES]
            acc = None
            for sub in range(2):
                h = g * SSM_HEADS_PER_GROUP + 2 * pr + sub
                seg = a_cum[:, h:h + 1] - a_cum_t[h:h + 1, :]
                dec = jnp.where(tril, jnp.exp(seg), 0.0)
                mm = (cbm * dec).astype(bf16)
                keep = lo_half if sub == 0 else jnp.logical_not(lo_half)
                part = _dot(mm, jnp.where(keep, xpair, 0.0).astype(bf16))
                acc = part if acc is None else acc + part
            pieces.append(acc)
        y_diag = jnp.concatenate(pieces, axis=1)
        prev = state_ref[g]
        y_off = _dot(cg, prev.astype(bf16)) * ea_x[:, g * hp:(g + 1) * hp]
        new_states = _dot_tn(bg, xw[:, g * hp:(g + 1) * hp])
        state_ref[g] = prev * ea_x[q - 1:q, g * hp:(g + 1) * hp] + new_states
        outs.append(y_diag + y_off)
    yv = jnp.concatenate(outs, axis=1) + dskip_ref[...] * xs
    z = s_ref[:, :SSM_WIDTH]
    yv = yv * (z * _sigmoid(z))
    ng = ng_ref[...]
    for g in range(SSM_GROUPS):
        yg = yv[:, g * hp:(g + 1) * hp]
        ms = jnp.mean(yg * yg, axis=1, keepdims=True)
        o_ref[:, g * hp:(g + 1) * hp] = (yg * lax.rsqrt(ms + RMS_EPS)
                                         * ng[:, g * hp:(g + 1) * hp]).astype(bf16)


def _ssd_mixer(ssm, conv_w, conv_b, dt_bias, a_log, d_skip, norm_g, bsz, seq):
    q = math.gcd(SSD_CHUNK, seq)
    nc = seq // q
    pad16 = lambda v: jnp.zeros((1, LANES), f32).at[0, :SSM_HEADS].set(v)
    head_of_lane = jnp.arange(SSM_WIDTH) // SSM_HEAD_DIM
    expand = (jnp.arange(LANES)[:, None] == head_of_lane[None, :]).astype(bf16)
    const = lambda shape: pl.BlockSpec(shape, lambda b, c: (0,) * len(shape))
    return pl.pallas_call(
        functools.partial(_ssd_kernel, q=q),
        grid=(bsz, nc),
        in_specs=[pl.BlockSpec((q, SSM_PAD), lambda b, c: (b * nc + c, 0)),
                  const((CONV_WIDTH, CONV_CH)), const((1, CONV_CH)), const((1, LANES)),
                  const((1, LANES)), const((1, SSM_WIDTH)), const((1, SSM_WIDTH)),
                  const((LANES, SSM_WIDTH))],
        out_specs=pl.BlockSpec((q, SSM_WIDTH), lambda b, c: (b * nc + c, 0)),
        out_shape=jax.ShapeDtypeStruct((bsz * seq, SSM_WIDTH), bf16),
        scratch_shapes=[pltpu.VMEM((q + 8, CONV_CH), f32),
                        pltpu.VMEM((SSM_GROUPS, SSM_STATE, SSM_HEADS_PER_GROUP * SSM_HEAD_DIM), f32)],
        compiler_params=_cparams(("parallel", "arbitrary")),
        name="ssd_mixer",
    )(ssm, conv_w, conv_b.reshape(1, CONV_CH), pad16(dt_bias), pad16(a_log),
      jnp.repeat(d_skip, SSM_HEAD_DIM).reshape(1, SSM_WIDTH), norm_g.reshape(1, SSM_WIDTH), expand)


def _layer_norm_rows(r, g, b):
    mu = jnp.mean(r, axis=1, keepdims=True)
    d = r - mu
    var = jnp.mean(d * d, axis=1, keepdims=True)
    return d * lax.rsqrt(var + LN_EPS) * g + b


def _outproj_kernel(oa_ref, ob_ref, oc_ref, w_ref, x_ref, gate_ref, g_ref, b_ref, o_ref, *, alpha):
    y = _dot(oa_ref[...], w_ref[0:A_WIDTH, :])
    y = y + _dot(ob_ref[...], w_ref[A_WIDTH:A_WIDTH + SSM_WIDTH, :])
    y = y + _dot(oc_ref[...], w_ref[A_WIDTH + SSM_WIDTH:, :])
    r = alpha * x_ref[...] + gate_ref[0] * y
    o_ref[...] = _layer_norm_rows(r, g_ref[...], b_ref[...])


def _out_proj(oa, ob, oc, w_out_b, x2, gate, ln_g, ln_b, alpha, seq):
    t, d = x2.shape
    tm = 256
    tpb = seq // tm
    rows = lambda w: pl.BlockSpec((tm, w), lambda i: (i, 0))
    return pl.pallas_call(
        functools.partial(_outproj_kernel, alpha=alpha),
        grid=(t // tm,),
        in_specs=[rows(A_WIDTH), rows(SSM_WIDTH), rows(DIFF_WIDTH), _full_spec(w_out_b.shape),
                  rows(d), pl.BlockSpec((1, 1, d), lambda i: (i // tpb, 0, 0)),
                  _full_spec((1, d)), _full_spec((1, d))],
        out_specs=rows(d),
        out_shape=jax.ShapeDtypeStruct((t, d), f32),
        compiler_params=_cparams(("parallel",)),
        name="out_proj_ln",
    )(oa, ob, oc, w_out_b, x2, gate, ln_g.reshape(1, d), ln_b.reshape(1, d))


def _router_kernel(x_ref, sc_ref, sh_ref, w1_ref, w2_ref, w3_ref, b_ref, h_ref, idx_ref, gate_ref,
                   *, n_experts):
    h = x_ref[...] * (1.0 + sc_ref[0]) + sh_ref[0]
    h_ref[...] = h.astype(bf16)
    logits = _dot_f32(h, (w1_ref[...], w2_ref[...], w3_ref[...])) + b_ref[...]
    lane = lax.broadcasted_iota(jnp.int32, logits.shape, 1)
    lane_f = lane.astype(f32)
    cur = jnp.where(lane < n_experts, logits, -jnp.inf)
    vals, idxs = [], []
    for _ in range(TOP_EXPERTS):
        m = jnp.max(cur, axis=1, keepdims=True)
        ix = jnp.min(jnp.where(cur == m, lane_f, float(LANES)), axis=1, keepdims=True)
        vals.append(m)
        idxs.append(ix)
        cur = jnp.where(lane_f == ix, -jnp.inf, cur)
    es = [jnp.exp(v - vals[0]) for v in vals]
    tot = es[0] + es[1] + es[2] + es[3]
    gates = jnp.zeros(logits.shape, f32)
    idx_o = jnp.zeros(logits.shape, f32)
    for k in range(TOP_EXPERTS):
        gates = jnp.where(lane == k, es[k] / tot, gates)
        idx_o = jnp.where(lane == k, idxs[k], idx_o)
    gate_ref[...] = gates
    idx_ref[...] = idx_o.astype(jnp.int32)


def _router(x2, scale, shift, router_w_l, router_b_l, seq):
    t, d = x2.shape
    n_experts = router_w_l.shape[1]
    tm = 256
    wp = jnp.zeros((d, LANES), f32).at[:, :n_experts].set(router_w_l)
    w1, w2, w3 = _split3(wp)
    bp = jnp.zeros((1, LANES), f32).at[0, :n_experts].set(router_b_l)
    rows = lambda w: pl.BlockSpec((tm, w), lambda i: (i, 0))
    return pl.pallas_call(
        functools.partial(_router_kernel, n_experts=n_experts),
        grid=(t // tm,),
        in_specs=_row_specs(tm, d, seq // tm) + [_full_spec((d, LANES))] * 3 + [_full_spec((1, LANES))],
        out_specs=[rows(d), rows(LANES), rows(LANES)],
        out_shape=[jax.ShapeDtypeStruct((t, d), bf16),
                   jax.ShapeDtypeStruct((t, LANES), jnp.int32),
                   jax.ShapeDtypeStruct((t, LANES), f32)],
        compiler_params=_cparams(("parallel",)),
        name="router",
    )(x2, scale, shift, w1, w2, w3, bp)


MOE_TM = 1024
MOE_TN = 512


def _expert_up_kernel(be_ref, nb_ref, x_ref, wg_ref, wu_ref, bg_ref, bu_ref, o_ref):
    live = pl.program_id(0) < nb_ref[0]

    @pl.when(live)
    def _():
        x = x_ref[...]
        g = _dot(x, wg_ref[0].astype(bf16)) + bg_ref[0]
        u = _dot(x, wu_ref[0].astype(bf16)) + bu_ref[0]
        g = jnp.minimum(g, SWIGLU_LIMIT)
        u = jnp.clip(u, -SWIGLU_LIMIT, SWIGLU_LIMIT)
        o_ref[...] = (g * _sigmoid(SWIGLU_ALPHA * g) * (u + 1.0)).astype(bf16)

    @pl.when(jnp.logical_not(live))
    def _():
        o_ref[...] = jnp.zeros(o_ref.shape, bf16)


def _expert_down_kernel(be_ref, nb_ref, a_ref, wd_ref, bd_ref, o_ref):
    live = pl.program_id(0) < nb_ref[0]

    @pl.when(live)
    def _():
        o_ref[...] = (_dot(a_ref[...], wd_ref[0].astype(bf16)) + bd_ref[0]).astype(bf16)

    @pl.when(jnp.logical_not(live))
    def _():
        o_ref[...] = jnp.zeros(o_ref.shape, bf16)


def _expert_ffn(xs, block_expert, n_live, w_gu, b_gu, w_down, b_down, layer):
    n_slots, d = xs.shape
    depth, n_exp, _, two_f = w_gu.shape
    fdim = two_f // 2
    tm = MOE_TM
    th, tn = min(MOE_TN, fdim), min(MOE_TN, d)
    nj, nd = fdim // th, d // tn
    nb = n_slots // tm
    rowmap = lambda b, j, be, nl: (jnp.minimum(b, nl[0] - 1), 0)
    jsel = lambda b, j, nl, n: jnp.where(b < nl[0], j, n - 1)
    up_spec = pltpu.PrefetchScalarGridSpec(
        num_scalar_prefetch=2,
        grid=(nb, nj),
        in_specs=[pl.BlockSpec((tm, d), rowmap),
                  pl.BlockSpec((None, 1, d, th), lambda b, j, be, nl: (layer, be[b], 0, jsel(b, j, nl, nj))),
                  pl.BlockSpec((None, 1, d, th), lambda b, j, be, nl: (layer, be[b], 0, nj + jsel(b, j, nl, nj))),
                  pl.BlockSpec((None, 1, 1, th), lambda b, j, be, nl: (layer, be[b], 0, jsel(b, j, nl, nj))),
                  pl.BlockSpec((None, 1, 1, th), lambda b, j, be, nl: (layer, be[b], 0, nj + jsel(b, j, nl, nj)))],
        out_specs=pl.BlockSpec((tm, th), lambda b, j, be, nl: (b, j)),
    )
    b_gu4 = b_gu.reshape(depth, n_exp, 1, two_f)
    act = pl.pallas_call(
        _expert_up_kernel,
        grid_spec=up_spec,
        out_shape=jax.ShapeDtypeStruct((n_slots, fdim), bf16),
        compiler_params=_cparams(("arbitrary", "arbitrary")),
        name="expert_up",
    )(block_expert, n_live, xs, w_gu, w_gu, b_gu4, b_gu4)
    down_spec = pltpu.PrefetchScalarGridSpec(
        num_scalar_prefetch=2,
        grid=(nb, nd),
        in_specs=[pl.BlockSpec((tm, fdim), rowmap),
                  pl.BlockSpec((None, 1, fdim, tn), lambda b, j, be, nl: (layer, be[b], 0, jsel(b, j, nl, nd))),
                  pl.BlockSpec((None, 1, 1, tn), lambda b, j, be, nl: (layer, be[b], 0, jsel(b, j, nl, nd)))],
        out_specs=pl.BlockSpec((tm, tn), lambda b, j, be, nl: (b, j)),
    )
    return pl.pallas_call(
        _expert_down_kernel,
        grid_spec=down_spec,
        out_shape=jax.ShapeDtypeStruct((n_slots, d), bf16),
        compiler_params=_cparams(("arbitrary", "arbitrary")),
        name="expert_down",
    )(block_expert, n_live, act, w_down, b_down.reshape(depth, n_exp, 1, d))


def _moe_dispatch(top_idx, n_tok, n_experts):
    tm = MOE_TM
    n_assign = n_tok * TOP_EXPERTS
    flat_e = top_idx.reshape(-1)
    onehot = (flat_e[:, None] == jnp.arange(n_experts, dtype=jnp.int32)[None, :]).astype(jnp.int32)
    before = jnp.cumsum(onehot, axis=0) - onehot
    rank = jnp.sum(before * onehot, axis=1)
    counts = jnp.sum(onehot, axis=0)
    padded = ((counts + tm - 1) // tm) * tm
    pend = jnp.cumsum(padded)
    pstart = pend - padded
    dest = pstart[flat_e] + rank
    n_blocks = -(-n_assign // tm) + n_experts
    n_slots = n_blocks * tm
    flat_tok = jnp.arange(n_assign, dtype=jnp.int32) // TOP_EXPERTS
    slot_tok = jnp.full((n_slots,), n_tok, jnp.int32).at[dest].set(flat_tok)
    n_live = (pend[-1] // tm).astype(jnp.int32)
    blk_start = jnp.minimum(jnp.arange(n_blocks, dtype=jnp.int32), n_live - 1) * tm
    block_expert = jnp.sum((pend[None, :] <= blk_start[:, None]).astype(jnp.int32), axis=1)
    block_expert = jnp.minimum(block_expert, n_experts - 1)
    return dest, slot_tok, block_expert, n_live.reshape(1)


def _final_kernel(x_ref, y_ref, rg_ref, gate_ref, g_ref, b_ref, o_ref, *, alpha):
    rg = rg_ref[...]
    tm = rg.shape[0]
    shape = (tm, TOP_EXPERTS * tm)
    rank = (lax.broadcasted_iota(jnp.int32, shape, 1)
            - TOP_EXPERTS * lax.broadcasted_iota(jnp.int32, shape, 0))
    gmat = jnp.zeros(shape, f32)
    for k in range(TOP_EXPERTS):
        gmat = jnp.where(rank == k, rg[:, k:k + 1], gmat)
    g_hi = gmat.astype(bf16)
    g_lo = (gmat - g_hi.astype(f32)).astype(bf16)
    yb = y_ref[...]
    y = _dot(g_hi, yb) + _dot(g_lo, yb)
    r = alpha * x_ref[...] + gate_ref[0] * y
    o_ref[...] = _layer_norm_rows(r, g_ref[...], b_ref[...])


def _final_ln(x2, y4, rgates, gate, ln_g, ln_b, alpha, seq):
    t, d = x2.shape
    tm = 256
    tpb = seq // tm
    return pl.pallas_call(
        functools.partial(_final_kernel, alpha=alpha),
        grid=(t // tm,),
        in_specs=[pl.BlockSpec((tm, d), lambda i: (i, 0)),
                  pl.BlockSpec((TOP_EXPERTS * tm, d), lambda i: (i, 0)),
                  pl.BlockSpec((tm, LANES), lambda i: (i, 0)),
                  pl.BlockSpec((1, 1, d), lambda i: (i // tpb, 0, 0)),
                  _full_spec((1, d)), _full_spec((1, d))],
        out_specs=pl.BlockSpec((tm, d), lambda i: (i, 0)),
        out_shape=jax.ShapeDtypeStruct((t, d), f32),
        compiler_params=_cparams(("parallel",)),
        name="moe_sum_ln",
    )(x2, y4, rgates, gate, ln_g.reshape(1, d), ln_b.reshape(1, d))


def kernel(x, c, positions, w_in, w_out, idx_ln_g, idx_ln_b, conv_w, conv_b, dt_bias, a_log,
           d_skip, ssm_norm_g, diff_lambda, diff_norm_g, ada_w, ada_b, ln_g, ln_b,
           router_w, router_b, w_gu, b_gu, w_down, b_down):
    bsz, seq, d = x.shape
    depth = w_in.shape[0]
    n_tok = bsz * seq
    n_experts = router_w.shape[-1]
    alpha = (2 * depth) ** 0.25

    mod = _ada_mod(c, ada_w, ada_b).reshape(depth, bsz, 6, 1, d)
    t128 = _rope_tables(positions, HEAD_DIM)
    t64 = _rope_tables(positions, IDX_DIM)
    x2 = x.reshape(n_tok, d)

    for layer in range(depth):
        shift_m, scale_m, gate_m, shift_f, scale_f, gate_f = (mod[layer, :, k] for k in range(6))
        w_dsa, w_ssm, w_diff, lng_p, lnb_p = _split_w_in(w_in[layer], idx_ln_g[layer], idx_ln_b[layer])

        qkv_a, iw = _proj_dsa(x2, scale_m, shift_m, w_dsa, t128, t64, lng_p, lnb_p, seq)
        out_a = _dsa_attention(qkv_a, iw, bsz, seq)
        ssm = _proj_ssm(x2, scale_m, shift_m, w_ssm, seq)
        out_b = _ssd_mixer(ssm, conv_w[layer], conv_b[layer], dt_bias[layer], a_log[layer],
                           d_skip[layer], ssm_norm_g[layer], bsz, seq)
        qkv_c = _proj_diff(x2, scale_m, shift_m, w_diff, t64, seq)
        lam_init = 0.8 - 0.6 * math.exp(-0.3 * layer)
        out_c = _diff_attention(qkv_c, diff_lambda[layer], diff_norm_g[layer], lam_init, bsz, seq)
        x2 = _out_proj(out_a, out_b, out_c, w_out[layer].astype(bf16), x2, gate_m,
                       ln_g[layer, 0], ln_b[layer, 0], alpha, seq)

        hf, top_idx, gates = _router(x2, scale_f, shift_f, router_w[layer], router_b[layer], seq)
        dest, slot_tok, block_expert, n_live = _moe_dispatch(top_idx[:, :TOP_EXPERTS], n_tok, n_experts)
        xs = hf[jnp.minimum(slot_tok, n_tok - 1)]
        ys = _expert_ffn(xs, block_expert, n_live, w_gu, b_gu, w_down, b_down, layer)
        y4 = ys[dest]
        x2 = _final_ln(x2, y4, gates, gate_f, ln_g[layer, 1], ln_b[layer, 1], alpha, seq)

    return x2.reshape(bsz, seq, d)
```
